```python
import math
import jax, jax.numpy as jnp
from jax import lax
import numpy as np

D_MODEL = 1024
BATCH = 16
SEQ = 4096
DEPTH = 1

MEM_LEN = 256
HEAD_DIM = 64
DIFF_HEADS = 4
DIFF_QK = HEAD_DIM
DIFF_V = 2 * HEAD_DIM
FOX_HEADS = 8
FOX_DIM = HEAD_DIM
MEM_HEADS = 4
MEM_DIM = 128
N_BRANCHES = 3
ROPE_THETA = 500000.0
ROPE_DIM = HEAD_DIM // 4
Q_BLOCK = 128
N_GROUPS = 4
EXPERTS_PER_GROUP = 8
N_EXPERTS = N_GROUPS * EXPERTS_PER_GROUP
EXPERT_TOP_K = 2
EXPERT_FF = 512
MOE_BLOCK = 128
EPS = 1e-6
NEG_INF = -1e30
IN_SPLITS = (DIFF_HEADS * 2 * DIFF_QK, DIFF_HEADS * 2 * DIFF_QK, DIFF_HEADS * DIFF_V,
             FOX_HEADS * FOX_DIM, FOX_HEADS * FOX_DIM, FOX_HEADS * FOX_DIM, FOX_HEADS,
             MEM_HEADS * MEM_DIM, N_BRANCHES * D_MODEL)
IN_COLS = sum(IN_SPLITS)

kernel_name = "hybrid_diff_fox_mem_hiermoe_layer"


def _rms(x, g):
    xf = x.astype(jnp.float32)
    y = xf * lax.rsqrt(jnp.mean(xf * xf, axis=-1, keepdims=True) + EPS)
    return (y * g.astype(jnp.float32)).astype(x.dtype)


def _rope_tables(positions):
    inv = ROPE_THETA ** (-jnp.arange(0, ROPE_DIM, 2, dtype=jnp.float32) / ROPE_DIM)
    ang = positions.astype(jnp.float32)[..., None] * inv
    return jnp.cos(ang), jnp.sin(ang)


def _partial_rope(t, cos, sin):
    shape = cos.shape[:2] + (1,) * (t.ndim - 3) + cos.shape[-1:]
    c = cos.reshape(shape).astype(t.dtype)
    s = sin.reshape(shape).astype(t.dtype)
    half = ROPE_DIM // 2
    t1, t2, rest = t[..., :half], t[..., half:ROPE_DIM], t[..., ROPE_DIM:]
    return jnp.concatenate([t1 * c - t2 * s, t2 * c + t1 * s, rest], axis=-1)


def _causal_mask(s0, s1):
    return jnp.arange(s1)[None, :] <= jnp.arange(s0, s1)[:, None]


def _diff_attention(q, k, v, cos, sin, qn_g, kn_g, lam_params, subln_g, lambda_init):
    B, S, _ = q.shape
    q = q.reshape(B, S, DIFF_HEADS, 2, DIFF_QK)
    k = k.reshape(B, S, DIFF_HEADS, 2, DIFF_QK)
    v = v.reshape(B, S, DIFF_HEADS, DIFF_V).transpose(0, 2, 1, 3)
    q = (_partial_rope(_rms(q, qn_g), cos, sin) * DIFF_QK ** -0.5).transpose(0, 2, 3, 1, 4)
    k = _partial_rope(_rms(k, kn_g), cos, sin).transpose(0, 2, 3, 1, 4)
    lp = lam_params.astype(jnp.float32)
    lam = jnp.exp(jnp.sum(lp[0] * lp[1])) - jnp.exp(jnp.sum(lp[2] * lp[3])) + lambda_init
    outs = []
    for i in range(S // Q_BLOCK):
        s0, s1 = i * Q_BLOCK, (i + 1) * Q_BLOCK
        sc = jnp.einsum('bhmqd,bhmkd->bhmqk', q[:, :, :, s0:s1], k[:, :, :, :s1]).astype(jnp.float32)
        p = jax.nn.softmax(jnp.where(_causal_mask(s0, s1), sc, NEG_INF), axis=-1)
        pd = p[:, :, 0] - lam * p[:, :, 1]
        outs.append(jnp.einsum('bhqk,bhkd->bhqd', pd.astype(v.dtype), v[:, :, :s1]))
    o = _rms(jnp.concatenate(outs, axis=2), subln_g) * (1.0 - lambda_init)
    return o.transpose(0, 2, 1, 3).reshape(B, S, DIFF_HEADS * DIFF_V)


def _forgetting_attention(q, k, v, f_logit, f_bias, qn_g, kn_g):
    B, S, _ = q.shape
    q = (_rms(q.reshape(B, S, FOX_HEADS, FOX_DIM), qn_g) * FOX_DIM ** -0.5).transpose(0, 2, 1, 3)
    k = _rms(k.reshape(B, S, FOX_HEADS, FOX_DIM), kn_g).transpose(0, 2, 1, 3)
    v = v.reshape(B, S, FOX_HEADS, FOX_DIM).transpose(0, 2, 1, 3)
    log_f = jax.nn.log_sigmoid(f_logit.astype(jnp.float32) + f_bias.astype(jnp.float32))
    cum = jnp.cumsum(log_f, axis=1).transpose(0, 2, 1)
    outs = []
    for i in range(S // Q_BLOCK):
        s0, s1 = i * Q_BLOCK, (i + 1) * Q_BLOCK
        sc = jnp.einsum('bhqd,bhkd->bhqk', q[:, :, s0:s1], k[:, :, :s1]).astype(jnp.float32)
        sc = sc + cum[:, :, s0:s1, None] - cum[:, :, None, :s1]
        p = jax.nn.softmax(jnp.where(_causal_mask(s0, s1), sc, NEG_INF), axis=-1)
        outs.append(jnp.einsum('bhqk,bhkd->bhqd', p.astype(v.dtype), v[:, :, :s1]))
    o = jnp.concatenate(outs, axis=2)
    return o.transpose(0, 2, 1, 3).reshape(B, S, FOX_HEADS * FOX_DIM)


def _memory_attention(q, mem_n, w_kv, qn_g, kn_g):
    B, S, _ = q.shape
    M = mem_n.shape[1]
    q = (_rms(q.reshape(B, S, MEM_HEADS, MEM_DIM), qn_g) * MEM_DIM ** -0.5).transpose(0, 2, 1, 3)
    kv = jnp.einsum('bmd,dc->bmc', mem_n, w_kv).reshape(B, M, 2, MEM_HEADS, MEM_DIM)
    k = _rms(kv[:, :, 0], kn_g).transpose(0, 2, 1, 3)
    v = kv[:, :, 1].transpose(0, 2, 1, 3)
    p = jax.nn.softmax(jnp.einsum('bhsd,bhmd->bhsm', q, k).astype(jnp.float32), axis=-1)
    o = jnp.einsum('bhsm,bhmd->bhsd', p.astype(v.dtype), v)
    return o.transpose(0, 2, 1, 3).reshape(B, S, MEM_HEADS * MEM_DIM)


def _hier_moe(h, w_rg, b_rg, w_re, b_re, w_up, w_down):
    B, S, D = h.shape
    T = B * S
    ht = h.reshape(T, D)
    g_logits = (ht @ w_rg).astype(jnp.float32) + b_rg.astype(jnp.float32)
    g_idx = jnp.argmax(g_logits, axis=-1).astype(jnp.int32)
    g_gate = jnp.take_along_axis(jax.nn.softmax(g_logits, axis=-1), g_idx[:, None], axis=-1)
    e_logits = ((ht @ w_re).astype(jnp.float32) + b_re.astype(jnp.float32)).reshape(T, N_GROUPS, EXPERTS_PER_GROUP)
    e_logits = jnp.take_along_axis(e_logits, g_idx[:, None, None], axis=1)[:, 0]
    top_logit, top_local = lax.top_k(e_logits, EXPERT_TOP_K)
    weights = g_gate * jax.nn.softmax(top_logit, axis=-1)
    expert_id = g_idx[:, None] * EXPERTS_PER_GROUP + top_local.astype(jnp.int32)
    TK = T * EXPERT_TOP_K
    e_flat = expert_id.reshape(TK)
    tok = jnp.repeat(jnp.arange(T, dtype=jnp.int32), EXPERT_TOP_K)
    w_flat = weights.reshape(TK)
    order = jnp.argsort(e_flat)
    e_s, tok_s, w_s = e_flat[order], tok[order], w_flat[order]
    counts = jnp.bincount(e_flat, length=N_EXPERTS)
    starts = jnp.cumsum(counts) - counts
    padded = (counts + MOE_BLOCK - 1) // MOE_BLOCK * MOE_BLOCK
    pends = jnp.cumsum(padded)
    pstarts = pends - padded
    dest = pstarts[e_s] + jnp.arange(TK, dtype=jnp.int32) - starts[e_s]
    n_blocks = (TK + N_EXPERTS * (MOE_BLOCK - 1) + MOE_BLOCK - 1) // MOE_BLOCK
    rows = jnp.zeros((n_blocks * MOE_BLOCK, D), h.dtype).at[dest].set(ht[tok_s])
    block_expert = jnp.minimum(
        jnp.searchsorted(pends, jnp.arange(n_blocks, dtype=jnp.int32) * MOE_BLOCK, side='right'),
        N_EXPERTS - 1)

    def expert_block(args):
        xb, e = args
        a, b = jnp.split(xb @ w_up[e], 2, axis=-1)
        return (jax.nn.silu(a) * b) @ w_down[e]

    ys = lax.map(expert_block, (rows.reshape(n_blocks, MOE_BLOCK, D), block_expert)).reshape(-1, D)
    contrib = (w_s[:, None] * ys[dest].astype(jnp.float32)).astype(h.dtype)
    out = jnp.zeros((T, D), h.dtype).at[tok_s].add(contrib)
    return out.reshape(B, S, D)


def setup_inputs(seed: int = 0) -> dict:
    key = jax.random.key(seed)
    ks = jax.random.split(key, 32)

    def nrm(k, shape, scale):
        return jax.random.normal(k, shape, jnp.float32) * scale

    def gain(k, shape):
        return 1.0 + 0.02 * jax.random.normal(k, shape, jnp.float32)

    L, D = DEPTH, D_MODEL
    positions = (jnp.arange(SEQ, dtype=jnp.int32)[None, :]
                 + jax.random.randint(ks[2], (BATCH, 1), 0, 1024, dtype=jnp.int32))
    return {
        "x": nrm(ks[0], (BATCH, SEQ, D), 1.0),
        "mem": nrm(ks[1], (BATCH, MEM_LEN, D), 1.0),
        "positions": positions,
        "attn_norm_g": gain(ks[3], (L, D)),
        "w_in": nrm(ks[4], (L, D, IN_COLS), D ** -0.5),
        "diff_qnorm_g": gain(ks[5], (L, DIFF_QK)),
        "diff_knorm_g": gain(ks[6], (L, DIFF_QK)),
        "diff_lambda": nrm(ks[7], (L, 4, DIFF_QK), 0.1),
        "diff_subln_g": gain(ks[8], (L, DIFF_V)),
        "fox_qnorm_g": gain(ks[9], (L, FOX_DIM)),
        "fox_knorm_g": gain(ks[10], (L, FOX_DIM)),
        "fox_forget_b": 3.0 + 0.5 * jax.random.normal(ks[11], (L, FOX_HEADS), jnp.float32),
        "mem_norm_g": gain(ks[12], (L, D)),
        "w_mem_kv": nrm(ks[13], (L, D, 2 * MEM_HEADS * MEM_DIM), D ** -0.5),
        "mem_qnorm_g": gain(ks[14], (L, MEM_DIM)),
        "mem_knorm_g": gain(ks[15], (L, MEM_DIM)),
        "w_o_diff": nrm(ks[16], (L, DIFF_HEADS * DIFF_V, D), (DIFF_HEADS * DIFF_V) ** -0.5),
        "w_o_fox": nrm(ks[17], (L, FOX_HEADS * FOX_DIM, D), (FOX_HEADS * FOX_DIM) ** -0.5),
        "w_o_mem": nrm(ks[18], (L, MEM_HEADS * MEM_DIM, D), (MEM_HEADS * MEM_DIM) ** -0.5),
        "w_out": nrm(ks[19], (L, D, D), D ** -0.5),
        "ffn_norm_g": gain(ks[20], (L, D)),
        "w_router_group": nrm(ks[21], (L, D, N_GROUPS), D ** -0.5),
        "b_router_group": nrm(ks[22], (L, N_GROUPS), 0.01),
        "w_router_expert": nrm(ks[23], (L, D, N_EXPERTS), D ** -0.5),
        "b_router_expert": nrm(ks[24], (L, N_EXPERTS), 0.01),
        "w_up": nrm(ks[25], (L, N_EXPERTS, D, 2 * EXPERT_FF), D ** -0.5),
        "w_down": nrm(ks[26], (L, N_EXPERTS, EXPERT_FF, D), EXPERT_FF ** -0.5),
    }


def reference(x, mem, positions, attn_norm_g, w_in, diff_qnorm_g, diff_knorm_g, diff_lambda,
              diff_subln_g, fox_qnorm_g, fox_knorm_g, fox_forget_b, mem_norm_g, w_mem_kv,
              mem_qnorm_g, mem_knorm_g, w_o_diff, w_o_fox, w_o_mem, w_out, ffn_norm_g,
              w_router_group, b_router_group, w_router_expert, b_router_expert, w_up, w_down):
    B, S, D = x.shape
    cos, sin = _rope_tables(positions)
    split_points = [int(i) for i in np.cumsum(IN_SPLITS)[:-1]]
    for l in range(DEPTH):
        lambda_init = 0.8 - 0.6 * math.exp(-0.3 * l)
        h = _rms(x, attn_norm_g[l])
        proj = jnp.einsum('bsd,dc->bsc', h, w_in[l])
        qa, ka, va, qb, kb, vb, fb, qc, gates = jnp.split(proj, split_points, axis=-1)
        ya = _diff_attention(qa, ka, va, cos, sin, diff_qnorm_g[l], diff_knorm_g[l],
                             diff_lambda[l], diff_subln_g[l], lambda_init)
        yb = _forgetting_attention(qb, kb, vb, fb, fox_forget_b[l], fox_qnorm_g[l], fox_knorm_g[l])
        yc = _memory_attention(qc, _rms(mem, mem_norm_g[l]), w_mem_kv[l], mem_qnorm_g[l], mem_knorm_g[l])
        g = jax.nn.sigmoid(gates.astype(jnp.float32)).astype(x.dtype).reshape(B, S, N_BRANCHES, D)
        merged = (g[:, :, 0] * (ya @ w_o_diff[l])
                  + g[:, :, 1] * (yb @ w_o_fox[l])
                  + g[:, :, 2] * (yc @ w_o_mem[l]))
        x = x + merged @ w_out[l]
        x = x + _hier_moe(_rms(x, ffn_norm_g[l]), w_router_group[l], b_router_group[l],
                          w_router_expert[l], b_router_expert[l], w_up[l], w_down[l])
    return x
```

```python
import functools
import math

import jax
import jax.numpy as jnp
from jax import lax
from jax.experimental import pallas as pl
from jax.experimental.pallas import tpu as pltpu

D_MODEL = 1024
HEAD_DIM = 64
DIFF_HEADS = 4
FOX_HEADS = 8
MEM_HEADS = 4
MEM_DIM = 128
ROPE_THETA = 500000.0
ROPE_DIM = HEAD_DIM // 4
N_GROUPS = 4
EXPERTS_PER_GROUP = 8
N_EXPERTS = N_GROUPS * EXPERTS_PER_GROUP
EXPERT_FF = 512
EPS = 1e-6
NEG_INF = -1e30
LAMBDA_INIT = 0.8 - 0.6 * math.exp(-0.3 * 0)

LANES = 128
VMEM_LIMIT_BYTES = 56 * 1024 * 1024

HW = 512
N_CLASSES = 256
N_LIVE_CLASSES = N_GROUPS * (EXPERTS_PER_GROUP * (EXPERTS_PER_GROUP - 1) // 2)
ROW_W = D_MODEL + LANES
EXPERT_BLOCK = 256


def _cparams(sem):
    return pltpu.CompilerParams(dimension_semantics=sem, vmem_limit_bytes=VMEM_LIMIT_BYTES)


def _rms_rows(x, g):
    return x * lax.rsqrt(jnp.mean(x * x, axis=-1, keepdims=True) + EPS) * g


def _dot(a, b):
    return jnp.dot(a, b, preferred_element_type=jnp.float32)


def _dot_nt(a, b):
    return lax.dot_general(a, b, (((1,), (1,)), ((), ())), preferred_element_type=jnp.float32)


def _mem_kv_kernel(mem_ref, g_ref, w_ref, kg_ref, k_out, v_out):
    mn = _rms_rows(mem_ref[...], g_ref[...]).astype(jnp.bfloat16)
    kv = _dot(mn, w_ref[...])
    for h in range(MEM_HEADS):
        kh = kv[:, h * MEM_DIM:(h + 1) * MEM_DIM]
        k_out[:, h * MEM_DIM:(h + 1) * MEM_DIM] = _rms_rows(kh, kg_ref[...]).astype(jnp.bfloat16)
    v_out[...] = kv[:, HW:].astype(jnp.bfloat16)


def _mem_kv(mem2, g, w_kv, kg, n_batch, mem_len):
    return pl.pallas_call(
        _mem_kv_kernel,
        grid=(n_batch,),
        in_specs=[
            pl.BlockSpec((mem_len, D_MODEL), lambda b: (b, 0)),
            pl.BlockSpec((1, D_MODEL), lambda b: (0, 0)),
            pl.BlockSpec((D_MODEL, 2 * HW), lambda b: (0, 0)),
            pl.BlockSpec((1, MEM_DIM), lambda b: (0, 0)),
        ],
        out_specs=[
            pl.BlockSpec((mem_len, HW), lambda b: (b, 0)),
            pl.BlockSpec((mem_len, HW), lambda b: (b, 0)),
        ],
        out_shape=[jax.ShapeDtypeStruct((n_batch * mem_len, HW), jnp.bfloat16)] * 2,
        compiler_params=_cparams(("arbitrary",)),
        name="mem_kv",
    )(mem2, g, w_kv, kg)


_C_QA, _C_KA, _C_VA, _C_QB, _C_KB, _C_VB, _C_QC, _C_G = (i * HW for i in range(8))
_MAIN_COLS = 7 * HW + 3 * D_MODEL
_NORM_CHUNK = 256


def _group_rms(r, bd, g):
    ss = _dot((r * r).astype(jnp.bfloat16), bd)
    return r * lax.rsqrt(ss * (1.0 / HEAD_DIM) + EPS) * g


def _in_proj_kernel(tiles_per_seq, x_ref, g_ref, w_ref, wf_ref, fb_ref, rc_ref, ra_ref, rb_ref, bd_ref, tri_ref,
                    gqa_ref, gka_ref, gqb_ref, gkb_ref, gqc_ref,
                    qa_o, ka_o, va_o, qb_o, kb_o, vb_o, qc_o, gate_o, cum_o, carry_ref):
    i = pl.program_id(0)
    h = _rms_rows(x_ref[...], g_ref[...]).astype(jnp.bfloat16)
    bd = bd_ref[...]
    nchunk = HW // _NORM_CHUNK
    rope_c = jnp.concatenate([rc_ref[...]] * (_NORM_CHUNK // LANES), axis=1)
    rope_a = jnp.concatenate([ra_ref[...]] * (_NORM_CHUNK // LANES), axis=1)
    rope_b = jnp.concatenate([rb_ref[...]] * (_NORM_CHUNK // LANES), axis=1)
    half = ROPE_DIM // 2

    def rope(y):
        return (y * rope_c + pltpu.roll(y, _NORM_CHUNK - half, axis=1) * rope_a
                + pltpu.roll(y, half, axis=1) * rope_b)

    def branch(col0, out, gain_ref, use_rope, scale):
        gain = gain_ref[...]
        for c in range(nchunk):
            lo = col0 + c * _NORM_CHUNK
            r = _dot(h, w_ref[:, lo:lo + _NORM_CHUNK])
            y = _group_rms(r, bd, gain)
            if use_rope:
                y = rope(y)
            if scale is not None:
                y = y * scale
            out[:, c * _NORM_CHUNK:(c + 1) * _NORM_CHUNK] = y.astype(jnp.bfloat16)

    branch(_C_QA, qa_o, gqa_ref, True, HEAD_DIM ** -0.5)
    branch(_C_KA, ka_o, gka_ref, True, None)
    branch(_C_QB, qb_o, gqb_ref, False, HEAD_DIM ** -0.5)
    branch(_C_KB, kb_o, gkb_ref, False, None)

    va_o[...] = _dot(h, w_ref[:, _C_VA:_C_VA + HW]).astype(jnp.bfloat16)
    vb_o[...] = _dot(h, w_ref[:, _C_VB:_C_VB + HW]).astype(jnp.bfloat16)

    gqc = gqc_ref[...]
    for hd in range(MEM_HEADS):
        lo = _C_QC + hd * MEM_DIM
        r = _dot(h, w_ref[:, lo:lo + MEM_DIM])
        qc_o[:, hd * MEM_DIM:(hd + 1) * MEM_DIM] = (_rms_rows(r, gqc) * MEM_DIM ** -0.5).astype(jnp.bfloat16)

    for c in range(3 * D_MODEL // HW):
        lo = _C_G + c * HW
        r = _dot(h, w_ref[:, lo:lo + HW])
        gate_o[:, c * HW:(c + 1) * HW] = jax.nn.sigmoid(r).astype(jnp.bfloat16)

    z = _dot(h, wf_ref[...]) + fb_ref[...]
    logf = -(jnp.maximum(-z, 0.0) + jnp.log1p(jnp.exp(-jnp.abs(z))))
    p1 = logf.astype(jnp.bfloat16)
    r1 = logf - p1.astype(jnp.float32)
    p2 = r1.astype(jnp.bfloat16)
    p3 = (r1 - p2.astype(jnp.float32)).astype(jnp.bfloat16)
    tri = tri_ref[...]

    @pl.when(i % tiles_per_seq == 0)
    def _():
        carry_ref[...] = jnp.zeros_like(carry_ref)

    cum = _dot(tri, p1) + _dot(tri, p2) + _dot(tri, p3) + carry_ref[...]
    cum_o[...] = cum
    carry_ref[...] = cum[cum.shape[0] - 1:, :]


def _in_proj(x2, g, w_main, w_f, fb, rope_c, rope_a, rope_b, gains, tm, tiles_per_seq):
    n_tok = x2.shape[0]
    bd = (lax.broadcasted_iota(jnp.int32, (_NORM_CHUNK, _NORM_CHUNK), 0) // HEAD_DIM
          == lax.broadcasted_iota(jnp.int32, (_NORM_CHUNK, _NORM_CHUNK), 1) // HEAD_DIM).astype(jnp.bfloat16)
    tri = (lax.broadcasted_iota(jnp.int32, (tm, tm), 1)
           <= lax.broadcasted_iota(jnp.int32, (tm, tm), 0)).astype(jnp.bfloat16)
    const = lambda shape: pl.BlockSpec(shape, lambda i: (0, 0))
    resident = lambda shape: pl.BlockSpec(shape, lambda i: (0, 0), pipeline_mode=pl.Buffered(1))
    row = lambda w: pl.BlockSpec((tm, w), lambda i: (i, 0))
    slab = jax.ShapeDtypeStruct((n_tok, HW), jnp.bfloat16)
    return pl.pallas_call(
        functools.partial(_in_proj_kernel, tiles_per_seq),
        grid=(n_tok // tm,),
        in_specs=[row(D_MODEL), const((1, D_MODEL)), resident((D_MODEL, _MAIN_COLS)), const((D_MODEL, LANES)),
                  const((1, LANES)), row(LANES), row(LANES), row(LANES),
                  const((_NORM_CHUNK, _NORM_CHUNK)), const((tm, tm)),
                  const((1, _NORM_CHUNK)), const((1, _NORM_CHUNK)), const((1, _NORM_CHUNK)),
                  const((1, _NORM_CHUNK)), const((1, MEM_DIM))],
        out_specs=[row(HW)] * 7 + [row(3 * D_MODEL), row(LANES)],
        out_shape=[slab] * 7 + [jax.ShapeDtypeStruct((n_tok, 3 * D_MODEL), jnp.bfloat16),
                                jax.ShapeDtypeStruct((n_tok, LANES), jnp.float32)],
        scratch_shapes=[pltpu.VMEM((1, LANES), jnp.float32)],
        compiler_params=_cparams(("arbitrary",)),
        name="in_proj",
    )(x2, g, w_main, w_f, fb, rope_c, rope_a, rope_b, bd, tri, *gains)


def _softmax_block(s, m_old, l_old, acc_old, v_blk, row_shift):
    m_new = jnp.maximum(m_old, jnp.max(s, axis=1, keepdims=True) + row_shift)
    p = jnp.exp(s - (m_new - row_shift))
    alpha = jnp.exp(m_old - m_new)
    l_new = alpha * l_old + jnp.sum(p, axis=1, keepdims=True)
    acc_new = alpha * acc_old + _dot(p.astype(jnp.bfloat16), v_blk)
    return m_new, l_new, acc_new


def _causal_pair_attention(q_ref, k_ref, v_ref, tq, kbias, qbias):
    qi = pl.program_id(2)
    q = q_ref[...]
    lane = lax.broadcasted_iota(jnp.int32, q.shape, 1)
    zero = jnp.zeros_like(q)
    qs = (jnp.where(lane < HEAD_DIM, q, zero), jnp.where(lane >= HEAD_DIM, q, zero))

    def block(kb, state, masked):
        k0 = pl.multiple_of(kb * tq, tq)
        k_blk = k_ref[pl.ds(k0, tq), :]
        v_blk = v_ref[pl.ds(k0, tq), :]
        out = []
        for which in range(2):
            m_old, l_old, acc_old = state[which]
            s = _dot_nt(qs[which], k_blk)
            kb_row = kbias(which, k0)
            if kb_row is not None:
                s = s - kb_row
            if masked:
                r = lax.broadcasted_iota(jnp.int32, s.shape, 0)
                c = lax.broadcasted_iota(jnp.int32, s.shape, 1)
                s = jnp.where(c <= r, s, NEG_INF)
            shift = qbias[which] if qbias[which] is not None else 0.0
            out.append(_softmax_block(s, m_old, l_old, acc_old, v_blk, shift))
        return tuple(out)

    init = tuple((jnp.full((tq, 1), NEG_INF, jnp.float32), jnp.zeros((tq, 1), jnp.float32),
                  jnp.zeros((tq, LANES), jnp.float32)) for _ in range(2))
    state = lax.fori_loop(0, qi, lambda kb, st: block(kb, st, False), init)
    state = block(qi, state, True)
    return [(l, acc) for (_, l, acc) in state]


def _fox_kernel(tq, q_ref, k_ref, v_ref, cq_ref, ck_ref, o_ref):
    j = pl.program_id(1)
    cq_all = cq_ref[...]
    lane = lax.broadcasted_iota(jnp.int32, cq_all.shape, 1)
    qbias = [jnp.sum(jnp.where(lane == 2 * j + w, cq_all, 0.0), axis=1, keepdims=True) for w in range(2)]

    def kbias(which, k0):
        return ck_ref[0, 0, which:which + 1, pl.ds(k0, tq)]

    (la, acca), (lb, accb) = _causal_pair_attention(q_ref, k_ref, v_ref, tq, kbias, qbias)
    lane_o = lax.broadcasted_iota(jnp.int32, acca.shape, 1)
    o_ref[...] = jnp.where(lane_o < HEAD_DIM, acca / la, accb / lb).astype(jnp.bfloat16)


def _fox_attn(qb, kb, vb, cum, cum_t, n_batch, seq, tq):
    nq = seq // tq
    return pl.pallas_call(
        functools.partial(_fox_kernel, tq),
        grid=(n_batch, FOX_HEADS // 2, nq),
        in_specs=[
            pl.BlockSpec((tq, LANES), lambda b, j, i: (b * nq + i, j)),
            pl.BlockSpec((seq, LANES), lambda b, j, i: (b, j)),
            pl.BlockSpec((seq, LANES), lambda b, j, i: (b, j)),
            pl.BlockSpec((tq, LANES), lambda b, j, i: (b * nq + i, 0)),
            pl.BlockSpec((1, 1, 2, seq), lambda b, j, i: (b, j, 0, 0)),
        ],
        out_specs=pl.BlockSpec((tq, LANES), lambda b, j, i: (b * nq + i, j)),
        out_shape=jax.ShapeDtypeStruct(qb.shape, jnp.bfloat16),
        compiler_params=_cparams(("arbitrary", "arbitrary", "arbitrary")),
        name="fox_attn",
    )(qb, kb, vb, cum, cum_t)


def _diff_kernel(tq, q_ref, k_ref, v_ref, lam_ref, g_ref, o_ref):
    lp = lam_ref[...]
    lam = (jnp.exp(jnp.sum(lp[0:1] * lp[1:2], axis=1, keepdims=True))
           - jnp.exp(jnp.sum(lp[2:3] * lp[3:4], axis=1, keepdims=True)) + LAMBDA_INIT)
    (l1, acc1), (l2, acc2) = _causal_pair_attention(q_ref, k_ref, v_ref, tq, lambda w, k0: None, [None, None])
    o = acc1 / l1 - lam * (acc2 / l2)
    o_ref[...] = (_rms_rows(o, g_ref[...]) * (1.0 - LAMBDA_INIT)).astype(jnp.bfloat16)


def _diff_attn(qa, ka, va, lam_params, subln_g, n_batch, seq, tq):
    nq = seq // tq
    return pl.pallas_call(
        functools.partial(_diff_kernel, tq),
        grid=(n_batch, DIFF_HEADS, nq),
        in_specs=[
            pl.BlockSpec((tq, LANES), lambda b, j, i: (b * nq + i, j)),
            pl.BlockSpec((seq, LANES), lambda b, j, i: (b, j)),
            pl.BlockSpec((seq, LANES), lambda b, j, i: (b, j)),
            pl.BlockSpec((4, HEAD_DIM), lambda b, j, i: (0, 0)),
            pl.BlockSpec((1, LANES), lambda b, j, i: (0, 0)),
        ],
        out_specs=pl.BlockSpec((tq, LANES), lambda b, j, i: (b * nq + i, j)),
        out_shape=jax.ShapeDtypeStruct(qa.shape, jnp.bfloat16),
        compiler_params=_cparams(("arbitrary", "arbitrary", "arbitrary")),
        name="diff_attn",
    )(qa, ka, va, lam_params, subln_g)


def _merge_kernel(x_ref, ya_ref, yb_ref, qc_ref, km_ref, vm_ref, gate_ref, wd_ref, wf_ref, wm_ref, wo_ref,
                  fg_ref, rhi_ref, rlo_ref, rb_ref, x1_o, h2_o, lg_o):
    heads = []
    for h in range(MEM_HEADS):
        sl = slice(h * MEM_DIM, (h + 1) * MEM_DIM)
        s = _dot_nt(qc_ref[:, sl], km_ref[:, sl])
        p = jnp.exp(s - jnp.max(s, axis=1, keepdims=True))
        o = _dot(p.astype(jnp.bfloat16), vm_ref[:, sl]) / jnp.sum(p, axis=1, keepdims=True)
        heads.append(o.astype(jnp.bfloat16))
    yc = jnp.concatenate(heads, axis=1)

    merged = (gate_ref[:, 0:D_MODEL].astype(jnp.float32) * _dot(ya_ref[...], wd_ref[...])
              + gate_ref[:, D_MODEL:2 * D_MODEL].astype(jnp.float32) * _dot(yb_ref[...], wf_ref[...])
              + gate_ref[:, 2 * D_MODEL:3 * D_MODEL].astype(jnp.float32) * _dot(yc, wm_ref[...]))
    x1 = x_ref[...] + _dot(merged.astype(jnp.bfloat16), wo_ref[...])
    x1_o[...] = x1
    h2 = _rms_rows(x1, fg_ref[...])
    h2_o[...] = h2
    a_hi = h2.astype(jnp.bfloat16)
    a_lo = (h2 - a_hi.astype(jnp.float32)).astype(jnp.bfloat16)
    lg_o[...] = (_dot(a_hi, rhi_ref[...]) + _dot(a_lo, rhi_ref[...]) + _dot(a_hi, rlo_ref[...])) + rb_ref[...]


def _merge(x2, ya, yb, qc, kmem, vmem, gate, wd, wf, wm, wo, fg, rhi, rlo, rb, tm, tiles_per_seq, mem_len):
    n_tok = x2.shape[0]
    const = lambda shape: pl.BlockSpec(shape, lambda i: (0, 0))
    row = lambda w: pl.BlockSpec((tm, w), lambda i: (i, 0))
    memspec = pl.BlockSpec((mem_len, HW), lambda i: (i // tiles_per_seq, 0))
    return pl.pallas_call(
        _merge_kernel,
        grid=(n_tok // tm,),
        in_specs=[row(D_MODEL), row(HW), row(HW), row(HW), memspec, memspec, row(3 * D_MODEL),
                  const((HW, D_MODEL)), const((HW, D_MODEL)), const((HW, D_MODEL)), const((D_MODEL, D_MODEL)),
                  const((1, D_MODEL)), const((D_MODEL, LANES)), const((D_MODEL, LANES)), const((1, LANES))],
        out_specs=[row(D_MODEL), row(D_MODEL), row(LANES)],
        out_shape=[jax.ShapeDtypeStruct((n_tok, D_MODEL), jnp.float32),
                   jax.ShapeDtypeStruct((n_tok, D_MODEL), jnp.float32),
                   jax.ShapeDtypeStruct((n_tok, LANES), jnp.float32)],
        compiler_params=_cparams(("arbitrary",)),
        name="merge",
    )(x2, ya, yb, qc, kmem, vmem, gate, wd, wf, wm, wo, fg, rhi, rlo, rb)


_EXPERT_LANE0 = N_GROUPS


def _route_kernel(lg_ref, tri_ref, w_o, idx_o, cnt_o, carry_ref):
    i = pl.program_id(0)

    @pl.when(i == 0)
    def _():
        carry_ref[...] = jnp.zeros_like(carry_ref)

    lg = lg_ref[...]
    lanei = lax.broadcasted_iota(jnp.int32, lg.shape, 1)
    lane = lanei.astype(jnp.float32)
    ninf = jnp.float32(-jnp.inf)
    big = jnp.float32(LANES)

    def first_max(vals):
        top = jnp.max(vals, axis=1, keepdims=True)
        return top, jnp.min(jnp.where(vals == top, lane, big), axis=1, keepdims=True)

    is_group = lanei < N_GROUPS
    gmax, gidx = first_max(jnp.where(is_group, lg, ninf))
    gate = 1.0 / jnp.sum(jnp.where(is_group, jnp.exp(lg - gmax), 0.0), axis=1, keepdims=True)
    lo = _EXPERT_LANE0 + EXPERTS_PER_GROUP * gidx
    el = jnp.where((lane >= lo) & (lane < lo + EXPERTS_PER_GROUP), lg, ninf)
    t1, i1 = first_max(el)
    t2, i2 = first_max(jnp.where(lane == i1, ninf, el))
    e2 = jnp.exp(t2 - t1)
    w1 = gate * (1.0 / (1.0 + e2))
    w2 = gate * (e2 / (1.0 + e2))
    first_low = i1 < i2
    ea = jnp.minimum(i1, i2) - lo
    eb = jnp.maximum(i1, i2) - lo
    wa = jnp.where(first_low, w1, w2)
    wb = jnp.where(first_low, w2, w1)
    cls = (EXPERTS_PER_GROUP * EXPERTS_PER_GROUP) * gidx + EXPERTS_PER_GROUP * ea + eb

    cls_lane = lax.broadcasted_iota(jnp.int32, (lg.shape[0], N_CLASSES), 1).astype(jnp.float32)
    onehot = cls_lane == cls
    before = _dot(tri_ref[...], jnp.where(onehot, 1.0, 0.0).astype(jnp.bfloat16)) + carry_ref[...]
    rank = jnp.sum(jnp.where(onehot, before, 0.0), axis=1, keepdims=True)
    carry_ref[...] = carry_ref[...] + jnp.sum(jnp.where(onehot, 1.0, 0.0), axis=0, keepdims=True)
    cnt_o[...] = carry_ref[...].astype(jnp.int32)

    w_o[...] = jnp.where(lanei == 0, wa, jnp.where(lanei == 1, wb, 0.0))
    idx_o[...] = jnp.where(lanei == 0, cls, jnp.where(lanei == 1, rank, 0.0)).astype(jnp.int32)


def _route(logits, tm):
    n_tok = logits.shape[0]
    tri = (lax.broadcasted_iota(jnp.int32, (tm, tm), 1)
           < lax.broadcasted_iota(jnp.int32, (tm, tm), 0)).astype(jnp.bfloat16)
    return pl.pallas_call(
        _route_kernel,
        grid=(n_tok // tm,),
        in_specs=[pl.BlockSpec((tm, LANES), lambda i: (i, 0)), pl.BlockSpec((tm, tm), lambda i: (0, 0))],
        out_specs=[pl.BlockSpec((tm, LANES), lambda i: (i, 0)), pl.BlockSpec((tm, LANES), lambda i: (i, 0)),
                   pl.BlockSpec((1, N_CLASSES), lambda i: (0, 0))],
        out_shape=[jax.ShapeDtypeStruct((n_tok, LANES), jnp.float32),
                   jax.ShapeDtypeStruct((n_tok, LANES), jnp.int32),
                   jax.ShapeDtypeStruct((1, N_CLASSES), jnp.int32)],
        scratch_shapes=[pltpu.VMEM((1, N_CLASSES), jnp.float32)],
        compiler_params=_cparams(("arbitrary",)),
        name="route",
    )(logits, tri)


def _dispatch_kernel(tm, dest_ref, h2_ref, w_ref, rows_in, rows_out, tile_ref, sem):
    del rows_in
    tile_ref[:, 0:D_MODEL] = h2_ref[...]
    tile_ref[:, D_MODEL:ROW_W] = w_ref[...]

    def row_copy(r):
        return pltpu.make_async_copy(tile_ref.at[pl.ds(r, 1)], rows_out.at[pl.ds(dest_ref[0, 0, r], 1)], sem)

    def start(r, c):
        row_copy(r).start()
        return c

    def wait(r, c):
        row_copy(r).wait()
        return c

    lax.fori_loop(0, tm, start, 0)
    lax.fori_loop(0, tm, wait, 0)


def _dispatch(dest3, h2, wslab, rows_zero, tm):
    n_tok = h2.shape[0]
    return pl.pallas_call(
        functools.partial(_dispatch_kernel, tm),
        grid=(n_tok // tm,),
        in_specs=[pl.BlockSpec((1, 1, tm), lambda i: (i, 0, 0), memory_space=pltpu.SMEM),
                  pl.BlockSpec((tm, D_MODEL), lambda i: (i, 0)),
                  pl.BlockSpec((tm, LANES), lambda i: (i, 0)),
                  pl.BlockSpec(memory_space=pl.ANY)],
        out_specs=pl.BlockSpec(memory_space=pl.ANY),
        out_shape=jax.ShapeDtypeStruct(rows_zero.shape, jnp.float32),
        scratch_shapes=[pltpu.VMEM((tm, ROW_W), jnp.float32), pltpu.SemaphoreType.DMA(())],
        input_output_aliases={3: 0},
        compiler_params=_cparams(("arbitrary",)),
        name="dispatch",
    )(dest3, h2, wslab, rows_zero)


def _combine_kernel(tm, dest_ref, x1_ref, y_ref, o_ref, buf_ref, sem):
    def row_copy(r):
        return pltpu.make_async_copy(y_ref.at[pl.ds(dest_ref[0, 0, r], 1)], buf_ref.at[pl.ds(r, 1)], sem)

    def start(r, c):
        row_copy(r).start()
        return c

    def wait(r, c):
        row_copy(r).wait()
        return c

    lax.fori_loop(0, tm, start, 0)
    lax.fori_loop(0, tm, wait, 0)
    o_ref[...] = x1_ref[...] + buf_ref[...]


def _combine(dest3, x1, y, tm):
    n_tok = x1.shape[0]
    return pl.pallas_call(
        functools.partial(_combine_kernel, tm),
        grid=(n_tok // tm,),
        in_specs=[pl.BlockSpec((1, 1, tm), lambda i: (i, 0, 0), memory_space=pltpu.SMEM),
                  pl.BlockSpec((tm, D_MODEL), lambda i: (i, 0)),
                  pl.BlockSpec(memory_space=pl.ANY)],
        out_specs=pl.BlockSpec((tm, D_MODEL), lambda i: (i, 0)),
        out_shape=jax.ShapeDtypeStruct(x1.shape, jnp.float32),
        scratch_shapes=[pltpu.VMEM((tm, D_MODEL), jnp.float32), pltpu.SemaphoreType.DMA(())],
        compiler_params=_cparams(("arbitrary",)),
        name="combine",
    )(dest3, x1, y)


def _expert_kernel(ea_ref, eb_ref, nused_ref, rows_ref, ua_ref, ub_ref, da_ref, db_ref, y_ref):
    b = pl.program_id(0)

    @pl.when(b < nused_ref[0])
    def _():
        h = rows_ref[:, 0:D_MODEL].astype(jnp.bfloat16)

        def expert(up_ref, down_ref, w):
            u = _dot(h, up_ref[0])
            act = (jax.nn.silu(u[:, 0:EXPERT_FF]) * u[:, EXPERT_FF:]).astype(jnp.bfloat16)
            return w * _dot(act, down_ref[0])

        y_ref[...] = (expert(ua_ref, da_ref, rows_ref[:, D_MODEL:D_MODEL + 1])
                      + expert(ub_ref, db_ref, rows_ref[:, D_MODEL + 1:D_MODEL + 2]))

    @pl.when(b >= nused_ref[0])
    def _():
        y_ref[...] = jnp.zeros_like(y_ref)


def _experts(blk_ea, blk_eb, nused, rows, w_up, w_down):
    n_blocks = rows.shape[0] // EXPERT_BLOCK
    grid_spec = pltpu.PrefetchScalarGridSpec(
        num_scalar_prefetch=3,
        grid=(n_blocks,),
        in_specs=[
            pl.BlockSpec((EXPERT_BLOCK, ROW_W), lambda b, ea, eb, nu: (b, 0)),
            pl.BlockSpec((1, D_MODEL, 2 * EXPERT_FF), lambda b, ea, eb, nu: (ea[b], 0, 0)),
            pl.BlockSpec((1, D_MODEL, 2 * EXPERT_FF), lambda b, ea, eb, nu: (eb[b], 0, 0)),
            pl.BlockSpec((1, EXPERT_FF, D_MODEL), lambda b, ea, eb, nu: (ea[b], 0, 0)),
            pl.BlockSpec((1, EXPERT_FF, D_MODEL), lambda b, ea, eb, nu: (eb[b], 0, 0)),
        ],
        out_specs=pl.BlockSpec((EXPERT_BLOCK, D_MODEL), lambda b, ea, eb, nu: (b, 0)),
    )
    return pl.pallas_call(
        _expert_kernel,
        grid_spec=grid_spec,
        out_shape=jax.ShapeDtypeStruct((rows.shape[0], D_MODEL), jnp.float32),
        compiler_params=_cparams(("arbitrary",)),
        name="experts",
    )(blk_ea, blk_eb, nused, rows, w_up, w_up, w_down, w_down)


def _rope_tables(positions):
    half = ROPE_DIM // 2
    inv = ROPE_THETA ** (-jnp.arange(0, ROPE_DIM, 2, dtype=jnp.float32) / ROPE_DIM)
    ang = positions.reshape(-1).astype(jnp.float32)[:, None] * inv
    cos, sin = jnp.cos(ang), jnp.sin(ang)
    ones = jnp.ones((ang.shape[0], HEAD_DIM - ROPE_DIM), jnp.float32)
    zeros = jnp.zeros_like(ones)
    zh = jnp.zeros_like(sin)
    c = jnp.concatenate([cos, cos, ones], axis=1)
    a = jnp.concatenate([-sin, zh, zeros], axis=1)
    b = jnp.concatenate([zh, sin, zeros], axis=1)
    rep = LANES // HEAD_DIM
    return jnp.tile(c, (1, rep)), jnp.tile(a, (1, rep)), jnp.tile(b, (1, rep))


def _pick_tile(n, pref):
    t = min(n, pref)
    while n % t:
        t //= 2
    return t


def kernel(x, mem, positions, attn_norm_g, w_in, diff_qnorm_g, diff_knorm_g, diff_lambda, diff_subln_g,
           fox_qnorm_g, fox_knorm_g, fox_forget_b, mem_norm_g, w_mem_kv, mem_qnorm_g, mem_knorm_g, w_o_diff,
           w_o_fox, w_o_mem, w_out, ffn_norm_g, w_router_group, b_router_group, w_router_expert,
           b_router_expert, w_up, w_down):
    n_batch, seq, _ = x.shape
    mem_len = mem.shape[1]
    n_tok = n_batch * seq
    bf = jnp.bfloat16
    tm = _pick_tile(seq, 512)
    tq = _pick_tile(seq, 512)
    tiles_per_seq = seq // tm
    x2 = x.reshape(n_tok, D_MODEL)

    wi = w_in[0]
    o = 0
    parts = {}
    for name, width in (("qa", HW), ("ka", HW), ("va", HW), ("qb", HW), ("kb", HW), ("vb", HW),
                        ("fb", FOX_HEADS), ("qc", HW), ("g", 3 * D_MODEL)):
        parts[name] = wi[:, o:o + width]
        o += width
    w_main = jnp.concatenate([parts[k] for k in ("qa", "ka", "va", "qb", "kb", "vb", "qc", "g")], axis=1).astype(bf)
    w_f = jnp.pad(parts["fb"], ((0, 0), (0, LANES - FOX_HEADS))).astype(bf)
    fb = jnp.pad(fox_forget_b[0], (0, LANES - FOX_HEADS)).reshape(1, LANES)
    tile_gain = lambda g, n: jnp.tile(g, n).reshape(1, -1)
    gains = (tile_gain(diff_qnorm_g[0], _NORM_CHUNK // HEAD_DIM), tile_gain(diff_knorm_g[0], _NORM_CHUNK // HEAD_DIM),
             tile_gain(fox_qnorm_g[0], _NORM_CHUNK // HEAD_DIM), tile_gain(fox_knorm_g[0], _NORM_CHUNK // HEAD_DIM),
             mem_qnorm_g[0].reshape(1, MEM_DIM))
    rope_c, rope_a, rope_b = _rope_tables(positions)

    kmem, vmem = _mem_kv(mem.reshape(n_batch * mem_len, D_MODEL), mem_norm_g[0].reshape(1, D_MODEL),
                         w_mem_kv[0].astype(bf), mem_knorm_g[0].reshape(1, MEM_DIM), n_batch, mem_len)

    qa, ka, va, qb, kb, vb, qc, gate, cum = _in_proj(
        x2, attn_norm_g[0].reshape(1, D_MODEL), w_main, w_f, fb, rope_c, rope_a, rope_b, gains, tm, tiles_per_seq)

    cum_t = cum[:, :FOX_HEADS].reshape(n_batch, seq, FOX_HEADS // 2, 2).transpose(0, 2, 3, 1)
    yb = _fox_attn(qb, kb, vb, cum, cum_t, n_batch, seq, tq)
    ya = _diff_attn(qa, ka, va, diff_lambda[0], diff_subln_g[0].reshape(1, LANES), n_batch, seq, tq)

    w_r = jnp.concatenate([w_router_group[0], w_router_expert[0]], axis=1)
    w_r = jnp.pad(w_r, ((0, 0), (0, LANES - w_r.shape[1])))
    r_hi = w_r.astype(bf)
    r_lo = (w_r - r_hi.astype(jnp.float32)).astype(bf)
    r_b = jnp.pad(jnp.concatenate([b_router_group[0], b_router_expert[0]]), (0, LANES - N_GROUPS - N_EXPERTS))
    x1, h2, logits = _merge(x2, ya, yb, qc, kmem, vmem, gate, w_o_diff[0].astype(bf), w_o_fox[0].astype(bf),
                            w_o_mem[0].astype(bf), w_out[0].astype(bf), ffn_norm_g[0].reshape(1, D_MODEL),
                            r_hi, r_lo, r_b.reshape(1, LANES), tm, tiles_per_seq, mem_len)

    wslab, islab, counts = _route(logits, tm)

    blk = EXPERT_BLOCK
    n_blocks = -(-(n_tok + N_LIVE_CLASSES * (blk - 1)) // blk)
    counts = counts[0]
    padded = (counts + blk - 1) // blk * blk
    pends = jnp.cumsum(padded)
    pstarts = pends - padded
    cls = islab[:, 0]
    dest = pstarts[cls] + islab[:, 1]
    blk_cls = jnp.minimum(jnp.searchsorted(pends, jnp.arange(n_blocks, dtype=jnp.int32) * blk, side="right"),
                          N_CLASSES - 1).astype(jnp.int32)
    per_group = EXPERTS_PER_GROUP * EXPERTS_PER_GROUP
    blk_ea = blk_cls // EXPERTS_PER_GROUP
    blk_eb = (blk_cls // per_group) * EXPERTS_PER_GROUP + blk_cls % EXPERTS_PER_GROUP
    nused = (pends[-1] // blk).astype(jnp.int32).reshape(1)
    dest3 = dest.astype(jnp.int32).reshape(n_tok // tm, 1, tm)

    rows = _dispatch(dest3, h2, wslab, jnp.zeros((n_blocks * blk, ROW_W), jnp.float32), tm)
    y = _experts(blk_ea, blk_eb, nused, rows, w_up[0].astype(bf), w_down[0].astype(bf))
    out = _combine(dest3, x1, y, tm)
    return out.reshape(x.shape)
```

```python
import functools
import math

import jax
import jax.numpy as jnp
from jax import lax
from jax.experimental import pallas as pl
from jax.experimental.pallas import tpu as pltpu

D_MODEL = 1024
HEAD_DIM = 64
DIFF_HEADS = 4
FOX_HEADS = 8
MEM_HEADS = 4
MEM_DIM = 128
ROPE_THETA = 500000.0
ROPE_DIM = HEAD_DIM // 4
N_GROUPS = 4
EXPERTS_PER_GROUP = 8
N_EXPERTS = N_GROUPS * EXPERTS_PER_GROUP
EXPERT_FF = 512
EPS = 1e-6
NEG_INF = -1e30
LAMBDA_INIT = 0.8 - 0.6 * math.exp(-0.3 * 0)
LOG2E = 1.4426950408889634

LANES = 128
VMEM_LIMIT_BYTES = 56 * 1024 * 1024

HW = 512
N_CLASSES = 256
N_LIVE_CLASSES = N_GROUPS * (EXPERTS_PER_GROUP * (EXPERTS_PER_GROUP - 1) // 2)
ROW_W = D_MODEL + LANES
EXPERT_BLOCK = 256


def _cparams(sem):
    return pltpu.CompilerParams(dimension_semantics=sem, vmem_limit_bytes=VMEM_LIMIT_BYTES)


def _rms_rows(x, g):
    return x * lax.rsqrt(jnp.mean(x * x, axis=-1, keepdims=True) + EPS) * g


def _dot(a, b):
    return jnp.dot(a, b, preferred_element_type=jnp.float32)


def _dot_nt(a, b):
    return lax.dot_general(a, b, (((1,), (1,)), ((), ())), preferred_element_type=jnp.float32)


def _mem_kv_kernel(mem_ref, g_ref, w_ref, kg_ref, k_out, v_out):
    mn = _rms_rows(mem_ref[...], g_ref[...]).astype(jnp.bfloat16)
    kv = _dot(mn, w_ref[...])
    for h in range(MEM_HEADS):
        kh = kv[:, h * MEM_DIM:(h + 1) * MEM_DIM]
        k_out[:, h * MEM_DIM:(h + 1) * MEM_DIM] = _rms_rows(kh, kg_ref[...]).astype(jnp.bfloat16)
    v_out[...] = kv[:, HW:].astype(jnp.bfloat16)


def _mem_kv(mem2, g, w_kv, kg, n_batch, mem_len):
    return pl.pallas_call(
        _mem_kv_kernel,
        grid=(n_batch,),
        in_specs=[
            pl.BlockSpec((mem_len, D_MODEL), lambda b: (b, 0)),
            pl.BlockSpec((1, D_MODEL), lambda b: (0, 0)),
            pl.BlockSpec((D_MODEL, 2 * HW), lambda b: (0, 0)),
            pl.BlockSpec((1, MEM_DIM), lambda b: (0, 0)),
        ],
        out_specs=[
            pl.BlockSpec((mem_len, HW), lambda b: (b, 0)),
            pl.BlockSpec((mem_len, HW), lambda b: (b, 0)),
        ],
        out_shape=[jax.ShapeDtypeStruct((n_batch * mem_len, HW), jnp.bfloat16)] * 2,
        compiler_params=_cparams(("arbitrary",)),
        name="mem_kv",
    )(mem2, g, w_kv, kg)


_C_QA, _C_KA, _C_VA, _C_QB, _C_KB, _C_VB, _C_QC, _C_G = (i * HW for i in range(8))
_MAIN_COLS = 7 * HW + 3 * D_MODEL
_NORM_CHUNK = 256
_HEADS_PER_CHUNK = _NORM_CHUNK // HEAD_DIM
FOX_SLAB = FOX_HEADS * LANES
BIAS_PARTS = 3


def _group_rms(r, bd, g):
    ss = _dot((r * r).astype(jnp.bfloat16), bd)
    return r * lax.rsqrt(ss * (1.0 / HEAD_DIM) + EPS) * g


def _in_proj_kernel(tiles_per_seq, x_ref, g_ref, w_ref, wf_ref, fb_ref, rc_ref, ra_ref, rb_ref, bd_ref, tri_ref,
                    place_ref, gqa_ref, gka_ref, gqb_ref, gkb_ref, gqc_ref,
                    qa_o, ka_o, va_o, qb_o, kb_o, vb_o, qc_o, gate_o, cum_o, carry_ref):
    i = pl.program_id(0)
    h = _rms_rows(x_ref[...], g_ref[...]).astype(jnp.bfloat16)
    bd = bd_ref[...]
    nchunk = HW // _NORM_CHUNK
    lane = lax.broadcasted_iota(jnp.int32, (h.shape[0], LANES), 1)

    def split3(v):
        p1 = v.astype(jnp.bfloat16)
        r1 = v - p1.astype(jnp.float32)
        p2 = r1.astype(jnp.bfloat16)
        return p1, p2, (r1 - p2.astype(jnp.float32)).astype(jnp.bfloat16)

    z = _dot(h, wf_ref[...]) + fb_ref[...]
    logf = -(jnp.maximum(-z, 0.0) + jnp.log1p(jnp.exp(-jnp.abs(z))))
    tri = tri_ref[...]

    @pl.when(i % tiles_per_seq == 0)
    def _():
        carry_ref[...] = jnp.zeros_like(carry_ref)

    cum = sum(_dot(tri, p) for p in split3(logf)) + carry_ref[...]
    cum_o[...] = cum
    carry_ref[...] = cum[cum.shape[0] - 1:, :]

    pieces = [p.astype(jnp.float32) for p in split3(jnp.where(lane < FOX_HEADS, cum * (-LOG2E), 0.0))]
    packed = pieces[0]
    for k in range(1, BIAS_PARTS):
        packed = packed + pltpu.roll(pieces[k], k * FOX_HEADS, axis=1)
    kbias = _dot(packed.astype(jnp.bfloat16), place_ref[...])
    qones = jnp.where((lane >= HEAD_DIM) & (lane < HEAD_DIM + BIAS_PARTS), 1.0, 0.0)

    rope_c = jnp.concatenate([rc_ref[...]] * (_NORM_CHUNK // LANES), axis=1)
    rope_a = jnp.concatenate([ra_ref[...]] * (_NORM_CHUNK // LANES), axis=1)
    rope_b = jnp.concatenate([rb_ref[...]] * (_NORM_CHUNK // LANES), axis=1)
    half = ROPE_DIM // 2

    def rope(y):
        return (y * rope_c + pltpu.roll(y, _NORM_CHUNK - half, axis=1) * rope_a
                + pltpu.roll(y, half, axis=1) * rope_b)

    def normed_chunk(col0, c, gain, use_rope, scale):
        lo = col0 + c * _NORM_CHUNK
        y = _group_rms(_dot(h, w_ref[:, lo:lo + _NORM_CHUNK]), bd, gain)
        if use_rope:
            y = rope(y)
        return y if scale is None else y * scale

    def packed_branch(col0, out, gain_ref, scale):
        gain = gain_ref[...]
        for c in range(nchunk):
            out[:, c * _NORM_CHUNK:(c + 1) * _NORM_CHUNK] = normed_chunk(col0, c, gain, True, scale).astype(jnp.bfloat16)

    def slab_branch(col0, out, gain_ref, scale, spare):
        gain = gain_ref[...]
        for c in range(nchunk):
            y = normed_chunk(col0, c, gain, False, scale)
            for hh in range(_HEADS_PER_CHUNK):
                src = y[:, (hh // 2) * LANES:(hh // 2 + 1) * LANES]
                if hh % 2:
                    src = pltpu.roll(src, HEAD_DIM, axis=1)
                head = c * _HEADS_PER_CHUNK + hh
                out[:, head * LANES:(head + 1) * LANES] = jnp.where(lane < HEAD_DIM, src, spare(head)).astype(jnp.bfloat16)

    q_scale = HEAD_DIM ** -0.5 * LOG2E
    packed_branch(_C_QA, qa_o, gqa_ref, q_scale)
    packed_branch(_C_KA, ka_o, gka_ref, None)
    slab_branch(_C_QB, qb_o, gqb_ref, q_scale, lambda head: qones)
    slab_branch(_C_KB, kb_o, gkb_ref, None, lambda head: kbias[:, head * LANES:(head + 1) * LANES])

    va_o[...] = _dot(h, w_ref[:, _C_VA:_C_VA + HW]).astype(jnp.bfloat16)
    vb_o[...] = _dot(h, w_ref[:, _C_VB:_C_VB + HW]).astype(jnp.bfloat16)

    gqc = gqc_ref[...]
    for hd in range(MEM_HEADS):
        lo = _C_QC + hd * MEM_DIM
        r = _dot(h, w_ref[:, lo:lo + MEM_DIM])
        qc_o[:, hd * MEM_DIM:(hd + 1) * MEM_DIM] = (_rms_rows(r, gqc) * MEM_DIM ** -0.5).astype(jnp.bfloat16)

    for c in range(3 * D_MODEL // HW):
        lo = _C_G + c * HW
        r = _dot(h, w_ref[:, lo:lo + HW])
        gate_o[:, c * HW:(c + 1) * HW] = jax.nn.sigmoid(r).astype(jnp.bfloat16)


def _in_proj(x2, g, w_main, w_f, fb, rope_c, rope_a, rope_b, gains, tm, tiles_per_seq):
    n_tok = x2.shape[0]
    bd = (lax.broadcasted_iota(jnp.int32, (_NORM_CHUNK, _NORM_CHUNK), 0) // HEAD_DIM
          == lax.broadcasted_iota(jnp.int32, (_NORM_CHUNK, _NORM_CHUNK), 1) // HEAD_DIM).astype(jnp.bfloat16)
    tri = (lax.broadcasted_iota(jnp.int32, (tm, tm), 1)
           <= lax.broadcasted_iota(jnp.int32, (tm, tm), 0)).astype(jnp.bfloat16)
    src = lax.broadcasted_iota(jnp.int32, (LANES, FOX_SLAB), 0)
    dst = lax.broadcasted_iota(jnp.int32, (LANES, FOX_SLAB), 1)
    place = ((src < BIAS_PARTS * FOX_HEADS)
             & (dst == (src % FOX_HEADS) * LANES + HEAD_DIM + src // FOX_HEADS)).astype(jnp.bfloat16)
    const = lambda shape: pl.BlockSpec(shape, lambda i: (0, 0))
    resident = lambda shape: pl.BlockSpec(shape, lambda i: (0, 0), pipeline_mode=pl.Buffered(1))
    row = lambda w: pl.BlockSpec((tm, w), lambda i: (i, 0))
    slab = lambda w: jax.ShapeDtypeStruct((n_tok, w), jnp.bfloat16)
    return pl.pallas_call(
        functools.partial(_in_proj_kernel, tiles_per_seq),
        grid=(n_tok // tm,),
        in_specs=[row(D_MODEL), const((1, D_MODEL)), resident((D_MODEL, _MAIN_COLS)), const((D_MODEL, LANES)),
                  const((1, LANES)), row(LANES), row(LANES), row(LANES),
                  const((_NORM_CHUNK, _NORM_CHUNK)), const((tm, tm)), const((LANES, FOX_SLAB)),
                  const((1, _NORM_CHUNK)), const((1, _NORM_CHUNK)), const((1, _NORM_CHUNK)),
                  const((1, _NORM_CHUNK)), const((1, MEM_DIM))],
        out_specs=[row(HW), row(HW), row(HW), row(FOX_SLAB), row(FOX_SLAB), row(HW), row(HW),
                   row(3 * D_MODEL), row(LANES)],
        out_shape=[slab(HW), slab(HW), slab(HW), slab(FOX_SLAB), slab(FOX_SLAB), slab(HW), slab(HW),
                   slab(3 * D_MODEL), jax.ShapeDtypeStruct((n_tok, LANES), jnp.float32)],
        scratch_shapes=[pltpu.VMEM((1, LANES), jnp.float32)],
        compiler_params=_cparams(("arbitrary",)),
        name="in_proj",
    )(x2, g, w_main, w_f, fb, rope_c, rope_a, rope_b, bd, tri, place, *gains)


FOX_V_ROWS = 80
DIFF_V_ROWS = 144


def _flash_transposed(qs, k_of, v_of, shifts, tq, tk, v_rows, finish):
    assert tk == tq
    qi = pl.program_id(2)
    chains = range(len(qs))

    def scores(kb, masked):
        k0 = pl.multiple_of(kb * tk, tk)
        out = []
        for w in chains:
            s = _dot_nt(k_of(w, k0), qs[w])
            if masked:
                key = lax.broadcasted_iota(jnp.int32, s.shape, 0)
                qry = lax.broadcasted_iota(jnp.int32, s.shape, 1)
                s = jnp.where(key <= qry, s, NEG_INF)
            out.append((s, jnp.max(s, axis=0, keepdims=True)))
        return tuple(out)

    def update(kb, sc, state):
        k0 = pl.multiple_of(kb * tk, tk)
        out = []
        for w in chains:
            (s, s_max), (m_old, acc_old) = sc[w], state[w]
            shift = shifts[w] if shifts[w] is not None else 0.0
            m_new = jnp.maximum(m_old, s_max + shift)
            p = jnp.exp2(s - (m_new - shift))
            acc_new = jnp.exp2(m_old - m_new) * acc_old + _dot(v_of(w, k0), p.astype(jnp.bfloat16))
            out.append((m_new, acc_new))
        return tuple(out)

    init = tuple((jnp.full((1, tq), NEG_INF, jnp.float32), jnp.zeros((v_rows, tq), jnp.float32)) for _ in chains)
    state = lax.fori_loop(0, qi, lambda kb, st: update(kb, scores(kb, False), st), init)
    finish([acc for (_, acc) in update(qi, scores(qi, True), state)])


ATTN_CHAINS = 4
FOX_STEP_HEADS = ATTN_CHAINS
DIFF_STEP_HEADS = ATTN_CHAINS // 2


def _fox_kernel(tq, tk, q_ref, k_ref, vt_ref, cq_ref, o_ref):
    heads = range(FOX_STEP_HEADS)
    qs = [q_ref[:, w * LANES:(w + 1) * LANES] for w in heads]
    shifts = [cq_ref[0, w // 2, w % 2:w % 2 + 1, :] * LOG2E for w in heads]

    def finish(accs):
        o_t = jnp.concatenate([a[0:HEAD_DIM] / a[HEAD_DIM:HEAD_DIM + 1] for a in accs], axis=0)
        o_ref[...] = o_t.T.astype(jnp.bfloat16)

    _flash_transposed(qs, lambda w, k0: k_ref[pl.ds(k0, tk), w * LANES:(w + 1) * LANES],
                      lambda w, k0: vt_ref[0, w, :, pl.ds(k0, tk)], shifts, tq, tk, FOX_V_ROWS, finish)


def _fox_attn(qb, kb, vbt, cum_t, n_batch, seq, tq, tk):
    nq = seq // tq
    hs = FOX_STEP_HEADS
    return pl.pallas_call(
        functools.partial(_fox_kernel, tq, tk),
        grid=(n_batch, FOX_HEADS // hs, nq),
        in_specs=[
            pl.BlockSpec((tq, hs * LANES), lambda b, j, i: (b * nq + i, j)),
            pl.BlockSpec((seq, hs * LANES), lambda b, j, i: (b, j)),
            pl.BlockSpec((1, hs, FOX_V_ROWS, seq), lambda b, j, i: (b, j, 0, 0)),
            pl.BlockSpec((1, hs // 2, 2, tq), lambda b, j, i: (b, j, 0, i)),
        ],
        out_specs=pl.BlockSpec((tq, hs * HEAD_DIM), lambda b, j, i: (b * nq + i, j)),
        out_shape=jax.ShapeDtypeStruct((n_batch * seq, HW), jnp.bfloat16),
        compiler_params=_cparams(("arbitrary", "arbitrary", "arbitrary")),
        name="fox_attn",
    )(qb, kb, vbt, cum_t)


def _diff_kernel(tq, tk, q_ref, k_ref, vt_ref, lam_ref, g_ref, o_ref):
    lp = lam_ref[...]
    lam = (jnp.exp(jnp.sum(lp[0:1] * lp[1:2], axis=1, keepdims=True))
           - jnp.exp(jnp.sum(lp[2:3] * lp[3:4], axis=1, keepdims=True)) + LAMBDA_INIT)
    lane = lax.broadcasted_iota(jnp.int32, (tq, LANES), 1)
    qs = []
    for hd in range(DIFF_STEP_HEADS):
        q = q_ref[:, hd * LANES:(hd + 1) * LANES]
        zero = jnp.zeros_like(q)
        qs += [jnp.where(lane < HEAD_DIM, q, zero), jnp.where(lane >= HEAD_DIM, q, zero)]
    dv = 2 * HEAD_DIM

    def finish(accs):
        outs = []
        for hd in range(DIFF_STEP_HEADS):
            acc1, acc2 = accs[2 * hd], accs[2 * hd + 1]
            o_t = acc1[0:dv] / acc1[dv:dv + 1] - lam * (acc2[0:dv] / acc2[dv:dv + 1])
            outs.append(o_t * lax.rsqrt(jnp.mean(o_t * o_t, axis=0, keepdims=True) + EPS))
        gain = jnp.concatenate([g_ref[...]] * DIFF_STEP_HEADS, axis=1)
        o_ref[...] = (jnp.concatenate(outs, axis=0).T * gain * (1.0 - LAMBDA_INIT)).astype(jnp.bfloat16)

    _flash_transposed(qs, lambda w, k0: k_ref[pl.ds(k0, tk), (w // 2) * LANES:(w // 2 + 1) * LANES],
                      lambda w, k0: vt_ref[0, w // 2, :, pl.ds(k0, tk)], [None] * len(qs), tq, tk, DIFF_V_ROWS,
                      finish)


def _diff_attn(qa, ka, vat, lam_params, subln_g, n_batch, seq, tq, tk):
    nq = seq // tq
    hs = DIFF_STEP_HEADS
    return pl.pallas_call(
        functools.partial(_diff_kernel, tq, tk),
        grid=(n_batch, DIFF_HEADS // hs, nq),
        in_specs=[
            pl.BlockSpec((tq, hs * LANES), lambda b, j, i: (b * nq + i, j)),
            pl.BlockSpec((seq, hs * LANES), lambda b, j, i: (b, j)),
            pl.BlockSpec((1, hs, DIFF_V_ROWS, seq), lambda b, j, i: (b, j, 0, 0)),
            pl.BlockSpec((4, HEAD_DIM), lambda b, j, i: (0, 0)),
            pl.BlockSpec((1, LANES), lambda b, j, i: (0, 0)),
        ],
        out_specs=pl.BlockSpec((tq, hs * LANES), lambda b, j, i: (b * nq + i, j)),
        out_shape=jax.ShapeDtypeStruct(qa.shape, jnp.bfloat16),
        compiler_params=_cparams(("arbitrary", "arbitrary", "arbitrary")),
        name="diff_attn",
    )(qa, ka, vat, lam_params, subln_g)


def _merge_kernel(x_ref, ya_ref, yb_ref, qc_ref, km_ref, vm_ref, gate_ref, wd_ref, wf_ref, wm_ref, wo_ref,
                  fg_ref, rhi_ref, rlo_ref, rb_ref, x1_o, h2_o, lg_o):
    heads = []
    for h in range(MEM_HEADS):
        sl = slice(h * MEM_DIM, (h + 1) * MEM_DIM)
        s = _dot_nt(qc_ref[:, sl], km_ref[:, sl])
        p = jnp.exp(s - jnp.max(s, axis=1, keepdims=True))
        o = _dot(p.astype(jnp.bfloat16), vm_ref[:, sl]) / jnp.sum(p, axis=1, keepdims=True)
        heads.append(o.astype(jnp.bfloat16))
    yc = jnp.concatenate(heads, axis=1)

    merged = (gate_ref[:, 0:D_MODEL].astype(jnp.float32) * _dot(ya_ref[...], wd_ref[...])
              + gate_ref[:, D_MODEL:2 * D_MODEL].astype(jnp.float32) * _dot(yb_ref[...], wf_ref[...])
              + gate_ref[:, 2 * D_MODEL:3 * D_MODEL].astype(jnp.float32) * _dot(yc, wm_ref[...]))
    x1 = x_ref[...] + _dot(merged.astype(jnp.bfloat16), wo_ref[...])
    x1_o[...] = x1
    h2 = _rms_rows(x1, fg_ref[...])
    h2_o[...] = h2
    a_hi = h2.astype(jnp.bfloat16)
    a_lo = (h2 - a_hi.astype(jnp.float32)).astype(jnp.bfloat16)
    lg_o[...] = (_dot(a_hi, rhi_ref[...]) + _dot(a_lo, rhi_ref[...]) + _dot(a_hi, rlo_ref[...])) + rb_ref[...]


def _merge(x2, ya, yb, qc, kmem, vmem, gate, wd, wf, wm, wo, fg, rhi, rlo, rb, tm, tiles_per_seq, mem_len):
    n_tok = x2.shape[0]
    const = lambda shape: pl.BlockSpec(shape, lambda i: (0, 0))
    row = lambda w: pl.BlockSpec((tm, w), lambda i: (i, 0))
    memspec = pl.BlockSpec((mem_len, HW), lambda i: (i // tiles_per_seq, 0))
    return pl.pallas_call(
        _merge_kernel,
        grid=(n_tok // tm,),
        in_specs=[row(D_MODEL), row(HW), row(HW), row(HW), memspec, memspec, row(3 * D_MODEL),
                  const((HW, D_MODEL)), const((HW, D_MODEL)), const((HW, D_MODEL)), const((D_MODEL, D_MODEL)),
                  const((1, D_MODEL)), const((D_MODEL, LANES)), const((D_MODEL, LANES)), const((1, LANES))],
        out_specs=[row(D_MODEL), row(D_MODEL), row(LANES)],
        out_shape=[jax.ShapeDtypeStruct((n_tok, D_MODEL), jnp.float32),
                   jax.ShapeDtypeStruct((n_tok, D_MODEL), jnp.float32),
                   jax.ShapeDtypeStruct((n_tok, LANES), jnp.float32)],
        compiler_params=_cparams(("arbitrary",)),
        name="merge",
    )(x2, ya, yb, qc, kmem, vmem, gate, wd, wf, wm, wo, fg, rhi, rlo, rb)


_EXPERT_LANE0 = N_GROUPS


def _route_kernel(lg_ref, tri_ref, w_o, idx_o, cnt_o, carry_ref):
    i = pl.program_id(0)

    @pl.when(i == 0)
    def _():
        carry_ref[...] = jnp.zeros_like(carry_ref)

    lg = lg_ref[...]
    lanei = lax.broadcasted_iota(jnp.int32, lg.shape, 1)
    lane = lanei.astype(jnp.float32)
    ninf = jnp.float32(-jnp.inf)
    big = jnp.float32(LANES)

    def first_max(vals):
        top = jnp.max(vals, axis=1, keepdims=True)
        return top, jnp.min(jnp.where(vals == top, lane, big), axis=1, keepdims=True)

    is_group = lanei < N_GROUPS
    gmax, gidx = first_max(jnp.where(is_group, lg, ninf))
    gate = 1.0 / jnp.sum(jnp.where(is_group, jnp.exp(lg - gmax), 0.0), axis=1, keepdims=True)
    lo = _EXPERT_LANE0 + EXPERTS_PER_GROUP * gidx
    el = jnp.where((lane >= lo) & (lane < lo + EXPERTS_PER_GROUP), lg, ninf)
    t1, i1 = first_max(el)
    t2, i2 = first_max(jnp.where(lane == i1, ninf, el))
    e2 = jnp.exp(t2 - t1)
    w1 = gate * (1.0 / (1.0 + e2))
    w2 = gate * (e2 / (1.0 + e2))
    first_low = i1 < i2
    ea = jnp.minimum(i1, i2) - lo
    eb = jnp.maximum(i1, i2) - lo
    wa = jnp.where(first_low, w1, w2)
    wb = jnp.where(first_low, w2, w1)
    cls = (EXPERTS_PER_GROUP * EXPERTS_PER_GROUP) * gidx + EXPERTS_PER_GROUP * ea + eb

    cls_lane = lax.broadcasted_iota(jnp.int32, (lg.shape[0], N_CLASSES), 1).astype(jnp.float32)
    onehot = cls_lane == cls
    before = _dot(tri_ref[...], jnp.where(onehot, 1.0, 0.0).astype(jnp.bfloat16)) + carry_ref[...]
    rank = jnp.sum(jnp.where(onehot, before, 0.0), axis=1, keepdims=True)
    carry_ref[...] = carry_ref[...] + jnp.sum(jnp.where(onehot, 1.0, 0.0), axis=0, keepdims=True)
    cnt_o[...] = carry_ref[...].astype(jnp.int32)

    w_o[...] = jnp.where(lanei == 0, wa, jnp.where(lanei == 1, wb, 0.0))
    idx_o[...] = jnp.where(lanei == 0, cls, jnp.where(lanei == 1, rank, 0.0)).astype(jnp.int32)


def _route(logits, tm):
    n_tok = logits.shape[0]
    tri = (lax.broadcasted_iota(jnp.int32, (tm, tm), 1)
           < lax.broadcasted_iota(jnp.int32, (tm, tm), 0)).astype(jnp.bfloat16)
    return pl.pallas_call(
        _route_kernel,
        grid=(n_tok // tm,),
        in_specs=[pl.BlockSpec((tm, LANES), lambda i: (i, 0)), pl.BlockSpec((tm, tm), lambda i: (0, 0))],
        out_specs=[pl.BlockSpec((tm, LANES), lambda i: (i, 0)), pl.BlockSpec((tm, LANES), lambda i: (i, 0)),
                   pl.BlockSpec((1, N_CLASSES), lambda i: (0, 0))],
        out_shape=[jax.ShapeDtypeStruct((n_tok, LANES), jnp.float32),
                   jax.ShapeDtypeStruct((n_tok, LANES), jnp.int32),
                   jax.ShapeDtypeStruct((1, N_CLASSES), jnp.int32)],
        scratch_shapes=[pltpu.VMEM((1, N_CLASSES), jnp.float32)],
        compiler_params=_cparams(("arbitrary",)),
        name="route",
    )(logits, tri)


def _dispatch_kernel(tm, dest_ref, h2_ref, w_ref, rows_in, rows_out, tile_ref, sem):
    del rows_in
    tile_ref[:, 0:D_MODEL] = h2_ref[...]
    tile_ref[:, D_MODEL:ROW_W] = w_ref[...]

    def row_copy(r):
        return pltpu.make_async_copy(tile_ref.at[pl.ds(r, 1)], rows_out.at[pl.ds(dest_ref[0, 0, r], 1)], sem)

    def start(r, c):
        row_copy(r).start()
        return c

    def wait(r, c):
        row_copy(r).wait()
        return c

    lax.fori_loop(0, tm, start, 0)
    lax.fori_loop(0, tm, wait, 0)


def _dispatch(dest3, h2, wslab, rows_zero, tm):
    n_tok = h2.shape[0]
    return pl.pallas_call(
        functools.partial(_dispatch_kernel, tm),
        grid=(n_tok // tm,),
        in_specs=[pl.BlockSpec((1, 1, tm), lambda i: (i, 0, 0), memory_space=pltpu.SMEM),
                  pl.BlockSpec((tm, D_MODEL), lambda i: (i, 0)),
                  pl.BlockSpec((tm, LANES), lambda i: (i, 0)),
                  pl.BlockSpec(memory_space=pl.ANY)],
        out_specs=pl.BlockSpec(memory_space=pl.ANY),
        out_shape=jax.ShapeDtypeStruct(rows_zero.shape, jnp.float32),
        scratch_shapes=[pltpu.VMEM((tm, ROW_W), jnp.float32), pltpu.SemaphoreType.DMA(())],
        input_output_aliases={3: 0},
        compiler_params=_cparams(("arbitrary",)),
        name="dispatch",
    )(dest3, h2, wslab, rows_zero)


def _combine_kernel(tm, dest_ref, x1_ref, y_ref, o_ref, buf_ref, sem):
    def row_copy(r):
        return pltpu.make_async_copy(y_ref.at[pl.ds(dest_ref[0, 0, r], 1)], buf_ref.at[pl.ds(r, 1)], sem)

    def start(r, c):
        row_copy(r).start()
        return c

    def wait(r, c):
        row_copy(r).wait()
        return c

    lax.fori_loop(0, tm, start, 0)
    lax.fori_loop(0, tm, wait, 0)
    o_ref[...] = x1_ref[...] + buf_ref[...]


def _combine(dest3, x1, y, tm):
    n_tok = x1.shape[0]
    return pl.pallas_call(
        functools.partial(_combine_kernel, tm),
        grid=(n_tok // tm,),
        in_specs=[pl.BlockSpec((1, 1, tm), lambda i: (i, 0, 0), memory_space=pltpu.SMEM),
                  pl.BlockSpec((tm, D_MODEL), lambda i: (i, 0)),
                  pl.BlockSpec(memory_space=pl.ANY)],
        out_specs=pl.BlockSpec((tm, D_MODEL), lambda i: (i, 0)),
        out_shape=jax.ShapeDtypeStruct(x1.shape, jnp.float32),
        scratch_shapes=[pltpu.VMEM((tm, D_MODEL), jnp.float32), pltpu.SemaphoreType.DMA(())],
        compiler_params=_cparams(("arbitrary",)),
        name="combine",
    )(dest3, x1, y)


def _expert_kernel(ea_ref, eb_ref, nused_ref, rows_ref, ua_ref, ub_ref, da_ref, db_ref, y_ref):
    b = pl.program_id(0)

    @pl.when(b < nused_ref[0])
    def _():
        h = rows_ref[:, 0:D_MODEL].astype(jnp.bfloat16)

        def expert(up_ref, down_ref, w):
            u = _dot(h, up_ref[0])
            act = (jax.nn.silu(u[:, 0:EXPERT_FF]) * u[:, EXPERT_FF:]).astype(jnp.bfloat16)
            return w * _dot(act, down_ref[0])

        y_ref[...] = (expert(ua_ref, da_ref, rows_ref[:, D_MODEL:D_MODEL + 1])
                      + expert(ub_ref, db_ref, rows_ref[:, D_MODEL + 1:D_MODEL + 2]))

    @pl.when(b >= nused_ref[0])
    def _():
        y_ref[...] = jnp.zeros_like(y_ref)


def _experts(blk_ea, blk_eb, nused, rows, w_up, w_down):
    n_blocks = rows.shape[0] // EXPERT_BLOCK
    grid_spec = pltpu.PrefetchScalarGridSpec(
        num_scalar_prefetch=3,
        grid=(n_blocks,),
        in_specs=[
            pl.BlockSpec((EXPERT_BLOCK, ROW_W), lambda b, ea, eb, nu: (b, 0)),
            pl.BlockSpec((1, D_MODEL, 2 * EXPERT_FF), lambda b, ea, eb, nu: (ea[b], 0, 0)),
            pl.BlockSpec((1, D_MODEL, 2 * EXPERT_FF), lambda b, ea, eb, nu: (eb[b], 0, 0)),
            pl.BlockSpec((1, EXPERT_FF, D_MODEL), lambda b, ea, eb, nu: (ea[b], 0, 0)),
            pl.BlockSpec((1, EXPERT_FF, D_MODEL), lambda b, ea, eb, nu: (eb[b], 0, 0)),
        ],
        out_specs=pl.BlockSpec((EXPERT_BLOCK, D_MODEL), lambda b, ea, eb, nu: (b, 0)),
    )
    return pl.pallas_call(
        _expert_kernel,
        grid_spec=grid_spec,
        out_shape=jax.ShapeDtypeStruct((rows.shape[0], D_MODEL), jnp.float32),
        compiler_params=_cparams(("arbitrary",)),
        name="experts",
    )(blk_ea, blk_eb, nused, rows, w_up, w_up, w_down, w_down)


def _rope_tables(positions):
    half = ROPE_DIM // 2
    inv = ROPE_THETA ** (-jnp.arange(0, ROPE_DIM, 2, dtype=jnp.float32) / ROPE_DIM)
    ang = positions.reshape(-1).astype(jnp.float32)[:, None] * inv
    cos, sin = jnp.cos(ang), jnp.sin(ang)
    ones = jnp.ones((ang.shape[0], HEAD_DIM - ROPE_DIM), jnp.float32)
    zeros = jnp.zeros_like(ones)
    zh = jnp.zeros_like(sin)
    c = jnp.concatenate([cos, cos, ones], axis=1)
    a = jnp.concatenate([-sin, zh, zeros], axis=1)
    b = jnp.concatenate([zh, sin, zeros], axis=1)
    rep = LANES // HEAD_DIM
    return jnp.tile(c, (1, rep)), jnp.tile(a, (1, rep)), jnp.tile(b, (1, rep))


def _pick_tile(n, pref):
    t = min(n, pref)
    while n % t:
        t //= 2
    return t


def kernel(x, mem, positions, attn_norm_g, w_in, diff_qnorm_g, diff_knorm_g, diff_lambda, diff_subln_g,
           fox_qnorm_g, fox_knorm_g, fox_forget_b, mem_norm_g, w_mem_kv, mem_qnorm_g, mem_knorm_g, w_o_diff,
           w_o_fox, w_o_mem, w_out, ffn_norm_g, w_router_group, b_router_group, w_router_expert,
           b_router_expert, w_up, w_down):
    n_batch, seq, _ = x.shape
    mem_len = mem.shape[1]
    n_tok = n_batch * seq
    bf = jnp.bfloat16
    tm = _pick_tile(seq, 512)
    tq = _pick_tile(seq, 512)
    tk = _pick_tile(tq, 512)
    tiles_per_seq = seq // tm
    x2 = x.reshape(n_tok, D_MODEL)

    wi = w_in[0]
    o = 0
    parts = {}
    for name, width in (("qa", HW), ("ka", HW), ("va", HW), ("qb", HW), ("kb", HW), ("vb", HW),
                        ("fb", FOX_HEADS), ("qc", HW), ("g", 3 * D_MODEL)):
        parts[name] = wi[:, o:o + width]
        o += width
    w_main = jnp.concatenate([parts[k] for k in ("qa", "ka", "va", "qb", "kb", "vb", "qc", "g")], axis=1).astype(bf)
    w_f = jnp.pad(parts["fb"], ((0, 0), (0, LANES - FOX_HEADS))).astype(bf)
    fb = jnp.pad(fox_forget_b[0], (0, LANES - FOX_HEADS)).reshape(1, LANES)
    tile_gain = lambda g, n: jnp.tile(g, n).reshape(1, -1)
    gains = (tile_gain(diff_qnorm_g[0], _NORM_CHUNK // HEAD_DIM), tile_gain(diff_knorm_g[0], _NORM_CHUNK // HEAD_DIM),
             tile_gain(fox_qnorm_g[0], _NORM_CHUNK // HEAD_DIM), tile_gain(fox_knorm_g[0], _NORM_CHUNK // HEAD_DIM),
             mem_qnorm_g[0].reshape(1, MEM_DIM))
    rope_c, rope_a, rope_b = _rope_tables(positions)

    kmem, vmem = _mem_kv(mem.reshape(n_batch * mem_len, D_MODEL), mem_norm_g[0].reshape(1, D_MODEL),
                         w_mem_kv[0].astype(bf), mem_knorm_g[0].reshape(1, MEM_DIM), n_batch, mem_len)

    qa, ka, va, qb, kb, vb, qc, gate, cum = _in_proj(
        x2, attn_norm_g[0].reshape(1, D_MODEL), w_main, w_f, fb, rope_c, rope_a, rope_b, gains, tm, tiles_per_seq)

    cum_t = cum[:, :FOX_HEADS].reshape(n_batch, seq, FOX_HEADS // 2, 2).transpose(0, 2, 3, 1)

    def values_t(v, heads, rows):
        vt = v.reshape(n_batch, seq, heads, HW // heads).transpose(0, 2, 3, 1)
        pad = rows - HW // heads - 1
        return jnp.concatenate([vt, jnp.ones((n_batch, heads, 1, seq), bf), jnp.zeros((n_batch, heads, pad, seq), bf)],
                               axis=2)

    yb = _fox_attn(qb, kb, values_t(vb, FOX_HEADS, FOX_V_ROWS), cum_t, n_batch, seq, tq, tk)
    ya = _diff_attn(qa, ka, values_t(va, DIFF_HEADS, DIFF_V_ROWS), diff_lambda[0], diff_subln_g[0].reshape(1, LANES),
                    n_batch, seq, tq, tk)

    w_r = jnp.concatenate([w_router_group[0], w_router_expert[0]], axis=1)
    w_r = jnp.pad(w_r, ((0, 0), (0, LANES - w_r.shape[1])))
    r_hi = w_r.astype(bf)
    r_lo = (w_r - r_hi.astype(jnp.float32)).astype(bf)
    r_b = jnp.pad(jnp.concatenate([b_router_group[0], b_router_expert[0]]), (0, LANES - N_GROUPS - N_EXPERTS))
    x1, h2, logits = _merge(x2, ya, yb, qc, kmem, vmem, gate, w_o_diff[0].astype(bf), w_o_fox[0].astype(bf),
                            w_o_mem[0].astype(bf), w_out[0].astype(bf), ffn_norm_g[0].reshape(1, D_MODEL),
                            r_hi, r_lo, r_b.reshape(1, LANES), tm, tiles_per_seq, mem_len)

    wslab, islab, counts = _route(logits, tm)

    blk = EXPERT_BLOCK
    n_blocks = -(-(n_tok + N_LIVE_CLASSES * (blk - 1)) // blk)
    counts = counts[0]
    padded = (counts + blk - 1) // blk * blk
    pends = jnp.cumsum(padded)
    pstarts = pends - padded
    cls = islab[:, 0]
    dest = pstarts[cls] + islab[:, 1]
    blk_cls = jnp.minimum(jnp.searchsorted(pends, jnp.arange(n_blocks, dtype=jnp.int32) * blk, side="right"),
                          N_CLASSES - 1).astype(jnp.int32)
    per_group = EXPERTS_PER_GROUP * EXPERTS_PER_GROUP
    blk_ea = blk_cls // EXPERTS_PER_GROUP
    blk_eb = (blk_cls // per_group) * EXPERTS_PER_GROUP + blk_cls % EXPERTS_PER_GROUP
    nused = (pends[-1] // blk).astype(jnp.int32).reshape(1)
    dest3 = dest.astype(jnp.int32).reshape(n_tok // tm, 1, tm)

    rows = _dispatch(dest3, h2, wslab, jnp.zeros((n_blocks * blk, ROW_W), jnp.float32), tm)
    y = _experts(blk_ea, blk_eb, nused, rows, w_up[0].astype(bf), w_down[0].astype(bf))
    out = _combine(dest3, x1, y, tm)
    return out.reshape(x.shape)
```

```python
import functools
import math

import jax
import jax.numpy as jnp
from jax import lax
from jax.experimental import pallas as pl
from jax.experimental.pallas import tpu as pltpu

D_MODEL = 1024
HEAD_DIM = 64
DIFF_HEADS = 4
FOX_HEADS = 8
MEM_HEADS = 4
MEM_DIM = 128
ROPE_THETA = 500000.0
ROPE_DIM = HEAD_DIM // 4
N_GROUPS = 4
EXPERTS_PER_GROUP = 8
N_EXPERTS = N_GROUPS * EXPERTS_PER_GROUP
EXPERT_FF = 512
EPS = 1e-6
NEG_INF = -1e30
LAMBDA_INIT = 0.8 - 0.6 * math.exp(-0.3 * 0)
LOG2E = 1.4426950408889634

LANES = 128
VMEM_LIMIT_BYTES = 56 * 1024 * 1024

HW = 512
N_CLASSES = 256
N_LIVE_CLASSES = N_GROUPS * (EXPERTS_PER_GROUP * (EXPERTS_PER_GROUP - 1) // 2)
ROW_W = D_MODEL + LANES
EXPERT_BLOCK = 256


def _cparams(sem):
    return pltpu.CompilerParams(dimension_semantics=sem, vmem_limit_bytes=VMEM_LIMIT_BYTES)


def _rms_rows(x, g):
    return x * lax.rsqrt(jnp.mean(x * x, axis=-1, keepdims=True) + EPS) * g


def _dot(a, b):
    return jnp.dot(a, b, preferred_element_type=jnp.float32)


def _dot_nt(a, b):
    return lax.dot_general(a, b, (((1,), (1,)), ((), ())), preferred_element_type=jnp.float32)


def _mem_kv_kernel(mem_ref, g_ref, w_ref, kg_ref, k_out, v_out):
    mn = _rms_rows(mem_ref[...], g_ref[...]).astype(jnp.bfloat16)
    kv = _dot(mn, w_ref[...])
    for h in range(MEM_HEADS):
        kh = kv[:, h * MEM_DIM:(h + 1) * MEM_DIM]
        k_out[:, h * MEM_DIM:(h + 1) * MEM_DIM] = _rms_rows(kh, kg_ref[...]).astype(jnp.bfloat16)
    v_out[...] = kv[:, HW:].astype(jnp.bfloat16)


def _mem_kv(mem2, g, w_kv, kg, n_batch, mem_len):
    return pl.pallas_call(
        _mem_kv_kernel,
        grid=(n_batch,),
        in_specs=[
            pl.BlockSpec((mem_len, D_MODEL), lambda b: (b, 0)),
            pl.BlockSpec((1, D_MODEL), lambda b: (0, 0)),
            pl.BlockSpec((D_MODEL, 2 * HW), lambda b: (0, 0)),
            pl.BlockSpec((1, MEM_DIM), lambda b: (0, 0)),
        ],
        out_specs=[
            pl.BlockSpec((mem_len, HW), lambda b: (b, 0)),
            pl.BlockSpec((mem_len, HW), lambda b: (b, 0)),
        ],
        out_shape=[jax.ShapeDtypeStruct((n_batch * mem_len, HW), jnp.bfloat16)] * 2,
        compiler_params=_cparams(("arbitrary",)),
        name="mem_kv",
    )(mem2, g, w_kv, kg)


_C_QA, _C_KA, _C_VA, _C_QB, _C_KB, _C_VB, _C_QC, _C_G = (i * HW for i in range(8))
_MAIN_COLS = 7 * HW + 3 * D_MODEL
_NORM_CHUNK = 256
_HEADS_PER_CHUNK = _NORM_CHUNK // HEAD_DIM
FOX_SLAB = FOX_HEADS * LANES
BIAS_PARTS = 3


def _group_rms(r, bd, g):
    ss = _dot((r * r).astype(jnp.bfloat16), bd)
    return r * lax.rsqrt(ss * (1.0 / HEAD_DIM) + EPS) * g


def _in_proj_kernel(tiles_per_seq, x_ref, g_ref, w_ref, wf_ref, fb_ref, rc_ref, ra_ref, rb_ref, bd_ref, tri_ref,
                    place_ref, gqa_ref, gka_ref, gqb_ref, gkb_ref, gqc_ref,
                    qa_o, ka_o, va_o, qb_o, kb_o, vb_o, qc_o, gate_o, cum_o, carry_ref):
    i = pl.program_id(0)
    h = _rms_rows(x_ref[...], g_ref[...]).astype(jnp.bfloat16)
    bd = bd_ref[...]
    nchunk = HW // _NORM_CHUNK
    lane = lax.broadcasted_iota(jnp.int32, (h.shape[0], LANES), 1)

    def split3(v):
        p1 = v.astype(jnp.bfloat16)
        r1 = v - p1.astype(jnp.float32)
        p2 = r1.astype(jnp.bfloat16)
        return p1, p2, (r1 - p2.astype(jnp.float32)).astype(jnp.bfloat16)

    z = _dot(h, wf_ref[...]) + fb_ref[...]
    logf = -(jnp.maximum(-z, 0.0) + jnp.log1p(jnp.exp(-jnp.abs(z))))
    tri = tri_ref[...]

    @pl.when(i % tiles_per_seq == 0)
    def _():
        carry_ref[...] = jnp.zeros_like(carry_ref)

    cum = sum(_dot(tri, p) for p in split3(logf)) + carry_ref[...]
    cum_o[...] = cum
    carry_ref[...] = cum[cum.shape[0] - 1:, :]

    pieces = [p.astype(jnp.float32) for p in split3(jnp.where(lane < FOX_HEADS, cum * (-LOG2E), 0.0))]
    packed = pieces[0]
    for k in range(1, BIAS_PARTS):
        packed = packed + pltpu.roll(pieces[k], k * FOX_HEADS, axis=1)
    kbias = _dot(packed.astype(jnp.bfloat16), place_ref[...])
    qones = jnp.where((lane >= HEAD_DIM) & (lane < HEAD_DIM + BIAS_PARTS), 1.0, 0.0)

    rope_c = jnp.concatenate([rc_ref[...]] * (_NORM_CHUNK // LANES), axis=1)
    rope_a = jnp.concatenate([ra_ref[...]] * (_NORM_CHUNK // LANES), axis=1)
    rope_b = jnp.concatenate([rb_ref[...]] * (_NORM_CHUNK // LANES), axis=1)
    half = ROPE_DIM // 2

    def rope(y):
        return (y * rope_c + pltpu.roll(y, _NORM_CHUNK - half, axis=1) * rope_a
                + pltpu.roll(y, half, axis=1) * rope_b)

    def normed_chunk(col0, c, gain, use_rope, scale):
        lo = col0 + c * _NORM_CHUNK
        y = _group_rms(_dot(h, w_ref[:, lo:lo + _NORM_CHUNK]), bd, gain)
        if use_rope:
            y = rope(y)
        return y if scale is None else y * scale

    def packed_branch(col0, out, gain_ref, scale):
        gain = gain_ref[...]
        for c in range(nchunk):
            out[:, c * _NORM_CHUNK:(c + 1) * _NORM_CHUNK] = normed_chunk(col0, c, gain, True, scale).astype(jnp.bfloat16)

    def slab_branch(col0, out, gain_ref, scale, spare):
        gain = gain_ref[...]
        for c in range(nchunk):
            y = normed_chunk(col0, c, gain, False, scale)
            for hh in range(_HEADS_PER_CHUNK):
                src = y[:, (hh // 2) * LANES:(hh // 2 + 1) * LANES]
                if hh % 2:
                    src = pltpu.roll(src, HEAD_DIM, axis=1)
                head = c * _HEADS_PER_CHUNK + hh
                out[:, head * LANES:(head + 1) * LANES] = jnp.where(lane < HEAD_DIM, src, spare(head)).astype(jnp.bfloat16)

    q_scale = HEAD_DIM ** -0.5 * LOG2E
    packed_branch(_C_QA, qa_o, gqa_ref, q_scale)
    packed_branch(_C_KA, ka_o, gka_ref, None)
    slab_branch(_C_QB, qb_o, gqb_ref, q_scale, lambda head: qones)
    slab_branch(_C_KB, kb_o, gkb_ref, None, lambda head: kbias[:, head * LANES:(head + 1) * LANES])

    va_o[...] = _dot(h, w_ref[:, _C_VA:_C_VA + HW]).astype(jnp.bfloat16)
    vb_o[...] = _dot(h, w_ref[:, _C_VB:_C_VB + HW]).astype(jnp.bfloat16)

    gqc = gqc_ref[...]
    for hd in range(MEM_HEADS):
        lo = _C_QC + hd * MEM_DIM
        r = _dot(h, w_ref[:, lo:lo + MEM_DIM])
        qc_o[:, hd * MEM_DIM:(hd + 1) * MEM_DIM] = (_rms_rows(r, gqc) * MEM_DIM ** -0.5).astype(jnp.bfloat16)

    for c in range(3 * D_MODEL // HW):
        lo = _C_G + c * HW
        r = _dot(h, w_ref[:, lo:lo + HW])
        gate_o[:, c * HW:(c + 1) * HW] = jax.nn.sigmoid(r).astype(jnp.bfloat16)


def _in_proj(x2, g, w_main, w_f, fb, rope_c, rope_a, rope_b, gains, tm, tiles_per_seq):
    n_tok = x2.shape[0]
    bd = (lax.broadcasted_iota(jnp.int32, (_NORM_CHUNK, _NORM_CHUNK), 0) // HEAD_DIM
          == lax.broadcasted_iota(jnp.int32, (_NORM_CHUNK, _NORM_CHUNK), 1) // HEAD_DIM).astype(jnp.bfloat16)
    tri = (lax.broadcasted_iota(jnp.int32, (tm, tm), 1)
           <= lax.broadcasted_iota(jnp.int32, (tm, tm), 0)).astype(jnp.bfloat16)
    src = lax.broadcasted_iota(jnp.int32, (LANES, FOX_SLAB), 0)
    dst = lax.broadcasted_iota(jnp.int32, (LANES, FOX_SLAB), 1)
    place = ((src < BIAS_PARTS * FOX_HEADS)
             & (dst == (src % FOX_HEADS) * LANES + HEAD_DIM + src // FOX_HEADS)).astype(jnp.bfloat16)
    const = lambda shape: pl.BlockSpec(shape, lambda i: (0, 0))
    resident = lambda shape: pl.BlockSpec(shape, lambda i: (0, 0), pipeline_mode=pl.Buffered(1))
    row = lambda w: pl.BlockSpec((tm, w), lambda i: (i, 0))
    slab = lambda w: jax.ShapeDtypeStruct((n_tok, w), jnp.bfloat16)
    return pl.pallas_call(
        functools.partial(_in_proj_kernel, tiles_per_seq),
        grid=(n_tok // tm,),
        in_specs=[row(D_MODEL), const((1, D_MODEL)), resident((D_MODEL, _MAIN_COLS)), const((D_MODEL, LANES)),
                  const((1, LANES)), row(LANES), row(LANES), row(LANES),
                  const((_NORM_CHUNK, _NORM_CHUNK)), const((tm, tm)), const((LANES, FOX_SLAB)),
                  const((1, _NORM_CHUNK)), const((1, _NORM_CHUNK)), const((1, _NORM_CHUNK)),
                  const((1, _NORM_CHUNK)), const((1, MEM_DIM))],
        out_specs=[row(HW), row(HW), row(HW), row(FOX_SLAB), row(FOX_SLAB), row(HW), row(HW),
                   row(3 * D_MODEL), row(LANES)],
        out_shape=[slab(HW), slab(HW), slab(HW), slab(FOX_SLAB), slab(FOX_SLAB), slab(HW), slab(HW),
                   slab(3 * D_MODEL), jax.ShapeDtypeStruct((n_tok, LANES), jnp.float32)],
        scratch_shapes=[pltpu.VMEM((1, LANES), jnp.float32)],
        compiler_params=_cparams(("arbitrary",)),
        name="in_proj",
    )(x2, g, w_main, w_f, fb, rope_c, rope_a, rope_b, bd, tri, place, *gains)


FOX_V_ROWS = 80
DIFF_V_ROWS = 144


def _flash_transposed(qs, k_of, v_of, shifts, tq, tk, v_rows, finish):
    assert tk == tq
    qi = pl.program_id(2)
    chains = range(len(qs))

    def scores(kb, masked):
        k0 = pl.multiple_of(kb * tk, tk)
        out = []
        for w in chains:
            s = _dot_nt(k_of(w, k0), qs[w])
            if masked:
                key = lax.broadcasted_iota(jnp.int32, s.shape, 0)
                qry = lax.broadcasted_iota(jnp.int32, s.shape, 1)
                s = jnp.where(key <= qry, s, NEG_INF)
            out.append((s, jnp.max(s, axis=0, keepdims=True)))
        return tuple(out)

    def update(kb, sc, state):
        k0 = pl.multiple_of(kb * tk, tk)
        out = []
        for w in chains:
            (s, s_max), (m_old, acc_old) = sc[w], state[w]
            shift = shifts[w] if shifts[w] is not None else 0.0
            m_new = jnp.maximum(m_old, s_max + shift)
            p = jnp.exp2(s - (m_new - shift))
            acc_new = jnp.exp2(m_old - m_new) * acc_old + _dot(v_of(w, k0), p.astype(jnp.bfloat16))
            out.append((m_new, acc_new))
        return tuple(out)

    init = tuple((jnp.full((1, tq), NEG_INF, jnp.float32), jnp.zeros((v_rows, tq), jnp.float32)) for _ in chains)
    state = lax.fori_loop(0, qi, lambda kb, st: update(kb, scores(kb, False), st), init)
    finish([acc for (_, acc) in update(qi, scores(qi, True), state)])


ATTN_CHAINS = 4


def _fill_values_t(v_ref, vt_ref, head_w, tk):
    n_heads, v_rows, seq = vt_ref.shape
    tail = lax.broadcasted_iota(jnp.int32, (v_rows - head_w, seq), 0)
    ones_row = jnp.where(tail == 0, 1.0, 0.0).astype(jnp.bfloat16)
    for h in range(n_heads):
        vt_ref[h, head_w:v_rows, :] = ones_row
    for c in range(seq // tk):
        blk_t = v_ref[c * tk:(c + 1) * tk, :].astype(jnp.float32).T
        for h in range(n_heads):
            vt_ref[h, 0:head_w, c * tk:(c + 1) * tk] = blk_t[h * head_w:(h + 1) * head_w].astype(jnp.bfloat16)

FOX_STEP_HEADS = ATTN_CHAINS
DIFF_STEP_HEADS = ATTN_CHAINS // 2


def _fox_kernel(tq, tk, q_ref, k_ref, v_ref, cq_ref, o_ref, vt_ref):
    heads = range(FOX_STEP_HEADS)

    @pl.when(pl.program_id(2) == 0)
    def _():
        _fill_values_t(v_ref, vt_ref, HEAD_DIM, tk)

    qs = [q_ref[:, w * LANES:(w + 1) * LANES] for w in heads]
    shifts = [cq_ref[0, w // 2, w % 2:w % 2 + 1, :] * LOG2E for w in heads]

    def finish(accs):
        o_t = jnp.concatenate([a[0:HEAD_DIM] / a[HEAD_DIM:HEAD_DIM + 1] for a in accs], axis=0)
        o_ref[...] = o_t.T.astype(jnp.bfloat16)

    _flash_transposed(qs, lambda w, k0: k_ref[pl.ds(k0, tk), w * LANES:(w + 1) * LANES],
                      lambda w, k0: vt_ref[w, :, pl.ds(k0, tk)], shifts, tq, tk, FOX_V_ROWS, finish)


def _fox_attn(qb, kb, vb, cum_t, n_batch, seq, tq, tk):
    nq = seq // tq
    hs = FOX_STEP_HEADS
    return pl.pallas_call(
        functools.partial(_fox_kernel, tq, tk),
        grid=(n_batch, FOX_HEADS // hs, nq),
        in_specs=[
            pl.BlockSpec((tq, hs * LANES), lambda b, j, i: (b * nq + i, j)),
            pl.BlockSpec((seq, hs * LANES), lambda b, j, i: (b, j)),
            pl.BlockSpec((seq, hs * HEAD_DIM), lambda b, j, i: (b, j)),
            pl.BlockSpec((1, hs // 2, 2, tq), lambda b, j, i: (b, j, 0, i)),
        ],
        out_specs=pl.BlockSpec((tq, hs * HEAD_DIM), lambda b, j, i: (b * nq + i, j)),
        out_shape=jax.ShapeDtypeStruct((n_batch * seq, HW), jnp.bfloat16),
        scratch_shapes=[pltpu.VMEM((hs, FOX_V_ROWS, seq), jnp.bfloat16)],
        compiler_params=_cparams(("arbitrary", "arbitrary", "arbitrary")),
        name="fox_attn",
    )(qb, kb, vb, cum_t)


def _diff_kernel(tq, tk, q_ref, k_ref, v_ref, lam_ref, g_ref, o_ref, vt_ref):
    @pl.when(pl.program_id(2) == 0)
    def _():
        _fill_values_t(v_ref, vt_ref, 2 * HEAD_DIM, tk)

    lp = lam_ref[...]
    lam = (jnp.exp(jnp.sum(lp[0:1] * lp[1:2], axis=1, keepdims=True))
           - jnp.exp(jnp.sum(lp[2:3] * lp[3:4], axis=1, keepdims=True)) + LAMBDA_INIT)
    lane = lax.broadcasted_iota(jnp.int32, (tq, LANES), 1)
    qs = []
    for hd in range(DIFF_STEP_HEADS):
        q = q_ref[:, hd * LANES:(hd + 1) * LANES]
        zero = jnp.zeros_like(q)
        qs += [jnp.where(lane < HEAD_DIM, q, zero), jnp.where(lane >= HEAD_DIM, q, zero)]
    dv = 2 * HEAD_DIM

    def finish(accs):
        outs = []
        for hd in range(DIFF_STEP_HEADS):
            acc1, acc2 = accs[2 * hd], accs[2 * hd + 1]
            o_t = acc1[0:dv] / acc1[dv:dv + 1] - lam * (acc2[0:dv] / acc2[dv:dv + 1])
            outs.append(o_t * lax.rsqrt(jnp.mean(o_t * o_t, axis=0, keepdims=True) + EPS))
        gain = jnp.concatenate([g_ref[...]] * DIFF_STEP_HEADS, axis=1)
        o_ref[...] = (jnp.concatenate(outs, axis=0).T * gain * (1.0 - LAMBDA_INIT)).astype(jnp.bfloat16)

    _flash_transposed(qs, lambda w, k0: k_ref[pl.ds(k0, tk), (w // 2) * LANES:(w // 2 + 1) * LANES],
                      lambda w, k0: vt_ref[w // 2, :, pl.ds(k0, tk)], [None] * len(qs), tq, tk, DIFF_V_ROWS,
                      finish)


def _diff_attn(qa, ka, va, lam_params, subln_g, n_batch, seq, tq, tk):
    nq = seq // tq
    hs = DIFF_STEP_HEADS
    return pl.pallas_call(
        functools.partial(_diff_kernel, tq, tk),
        grid=(n_batch, DIFF_HEADS // hs, nq),
        in_specs=[
            pl.BlockSpec((tq, hs * LANES), lambda b, j, i: (b * nq + i, j)),
            pl.BlockSpec((seq, hs * LANES), lambda b, j, i: (b, j)),
            pl.BlockSpec((seq, hs * LANES), lambda b, j, i: (b, j)),
            pl.BlockSpec((4, HEAD_DIM), lambda b, j, i: (0, 0)),
            pl.BlockSpec((1, LANES), lambda b, j, i: (0, 0)),
        ],
        out_specs=pl.BlockSpec((tq, hs * LANES), lambda b, j, i: (b * nq + i, j)),
        out_shape=jax.ShapeDtypeStruct(qa.shape, jnp.bfloat16),
        scratch_shapes=[pltpu.VMEM((hs, DIFF_V_ROWS, seq), jnp.bfloat16)],
        compiler_params=_cparams(("arbitrary", "arbitrary", "arbitrary")),
        name="diff_attn",
    )(qa, ka, va, lam_params, subln_g)


def _merge_kernel(x_ref, ya_ref, yb_ref, qc_ref, km_ref, vm_ref, gate_ref, wd_ref, wf_ref, wm_ref, wo_ref,
                  fg_ref, rhi_ref, rlo_ref, rb_ref, x1_o, h2_o, lg_o):
    heads = []
    for h in range(MEM_HEADS):
        sl = slice(h * MEM_DIM, (h + 1) * MEM_DIM)
        s = _dot_nt(qc_ref[:, sl], km_ref[:, sl])
        p = jnp.exp(s - jnp.max(s, axis=1, keepdims=True))
        o = _dot(p.astype(jnp.bfloat16), vm_ref[:, sl]) / jnp.sum(p, axis=1, keepdims=True)
        heads.append(o.astype(jnp.bfloat16))
    yc = jnp.concatenate(heads, axis=1)

    merged = (gate_ref[:, 0:D_MODEL].astype(jnp.float32) * _dot(ya_ref[...], wd_ref[...])
              + gate_ref[:, D_MODEL:2 * D_MODEL].astype(jnp.float32) * _dot(yb_ref[...], wf_ref[...])
              + gate_ref[:, 2 * D_MODEL:3 * D_MODEL].astype(jnp.float32) * _dot(yc, wm_ref[...]))
    x1 = x_ref[...] + _dot(merged.astype(jnp.bfloat16), wo_ref[...])
    x1_o[...] = x1
    h2 = _rms_rows(x1, fg_ref[...])
    h2_o[...] = h2
    a_hi = h2.astype(jnp.bfloat16)
    a_lo = (h2 - a_hi.astype(jnp.float32)).astype(jnp.bfloat16)
    lg_o[...] = (_dot(a_hi, rhi_ref[...]) + _dot(a_lo, rhi_ref[...]) + _dot(a_hi, rlo_ref[...])) + rb_ref[...]


def _merge(x2, ya, yb, qc, kmem, vmem, gate, wd, wf, wm, wo, fg, rhi, rlo, rb, tm, tiles_per_seq, mem_len):
    n_tok = x2.shape[0]
    const = lambda shape: pl.BlockSpec(shape, lambda i: (0, 0))
    row = lambda w: pl.BlockSpec((tm, w), lambda i: (i, 0))
    memspec = pl.BlockSpec((mem_len, HW), lambda i: (i // tiles_per_seq, 0))
    return pl.pallas_call(
        _merge_kernel,
        grid=(n_tok // tm,),
        in_specs=[row(D_MODEL), row(HW), row(HW), row(HW), memspec, memspec, row(3 * D_MODEL),
                  const((HW, D_MODEL)), const((HW, D_MODEL)), const((HW, D_MODEL)), const((D_MODEL, D_MODEL)),
                  const((1, D_MODEL)), const((D_MODEL, LANES)), const((D_MODEL, LANES)), const((1, LANES))],
        out_specs=[row(D_MODEL), row(D_MODEL), row(LANES)],
        out_shape=[jax.ShapeDtypeStruct((n_tok, D_MODEL), jnp.float32),
                   jax.ShapeDtypeStruct((n_tok, D_MODEL), jnp.float32),
                   jax.ShapeDtypeStruct((n_tok, LANES), jnp.float32)],
        compiler_params=_cparams(("arbitrary",)),
        name="merge",
    )(x2, ya, yb, qc, kmem, vmem, gate, wd, wf, wm, wo, fg, rhi, rlo, rb)


_EXPERT_LANE0 = N_GROUPS


def _route_kernel(lg_ref, tri_ref, w_o, cls_o, rank_o, cnt_o, carry_ref):
    i = pl.program_id(0)

    @pl.when(i == 0)
    def _():
        carry_ref[...] = jnp.zeros_like(carry_ref)

    lg = lg_ref[...]
    lanei = lax.broadcasted_iota(jnp.int32, lg.shape, 1)
    lane = lanei.astype(jnp.float32)
    ninf = jnp.float32(-jnp.inf)
    big = jnp.float32(LANES)

    def first_max(vals):
        top = jnp.max(vals, axis=1, keepdims=True)
        return top, jnp.min(jnp.where(vals == top, lane, big), axis=1, keepdims=True)

    is_group = lanei < N_GROUPS
    gmax, gidx = first_max(jnp.where(is_group, lg, ninf))
    gate = 1.0 / jnp.sum(jnp.where(is_group, jnp.exp(lg - gmax), 0.0), axis=1, keepdims=True)
    lo = _EXPERT_LANE0 + EXPERTS_PER_GROUP * gidx
    el = jnp.where((lane >= lo) & (lane < lo + EXPERTS_PER_GROUP), lg, ninf)
    t1, i1 = first_max(el)
    t2, i2 = first_max(jnp.where(lane == i1, ninf, el))
    e2 = jnp.exp(t2 - t1)
    w1 = gate * (1.0 / (1.0 + e2))
    w2 = gate * (e2 / (1.0 + e2))
    first_low = i1 < i2
    ea = jnp.minimum(i1, i2) - lo
    eb = jnp.maximum(i1, i2) - lo
    wa = jnp.where(first_low, w1, w2)
    wb = jnp.where(first_low, w2, w1)
    cls = (EXPERTS_PER_GROUP * EXPERTS_PER_GROUP) * gidx + EXPERTS_PER_GROUP * ea + eb

    cls_lane = lax.broadcasted_iota(jnp.int32, (lg.shape[0], N_CLASSES), 1).astype(jnp.float32)
    onehot = cls_lane == cls
    before = _dot(tri_ref[...], jnp.where(onehot, 1.0, 0.0).astype(jnp.bfloat16)) + carry_ref[...]
    rank = jnp.sum(jnp.where(onehot, before, 0.0), axis=1, keepdims=True)
    carry_ref[...] = carry_ref[...] + jnp.sum(jnp.where(onehot, 1.0, 0.0), axis=0, keepdims=True)
    cnt_o[...] = carry_ref[...].astype(jnp.int32)

    w_o[...] = jnp.where(lanei == 0, wa, jnp.where(lanei == 1, wb, 0.0))
    idx_t = jnp.where(lanei == 0, cls, jnp.where(lanei == 1, rank, 0.0)).T
    cls_o[0] = idx_t[0:1, :].astype(jnp.int32)
    rank_o[0] = idx_t[1:2, :].astype(jnp.int32)


def _route(logits, tm):
    n_tok = logits.shape[0]
    nt = n_tok // tm
    tri = (lax.broadcasted_iota(jnp.int32, (tm, tm), 1)
           < lax.broadcasted_iota(jnp.int32, (tm, tm), 0)).astype(jnp.bfloat16)
    return pl.pallas_call(
        _route_kernel,
        grid=(nt,),
        in_specs=[pl.BlockSpec((tm, LANES), lambda i: (i, 0)), pl.BlockSpec((tm, tm), lambda i: (0, 0))],
        out_specs=[pl.BlockSpec((tm, LANES), lambda i: (i, 0)), pl.BlockSpec((1, 1, tm), lambda i: (i, 0, 0)),
                   pl.BlockSpec((1, 1, tm), lambda i: (i, 0, 0)), pl.BlockSpec((1, N_CLASSES), lambda i: (0, 0))],
        out_shape=[jax.ShapeDtypeStruct((n_tok, LANES), jnp.float32),
                   jax.ShapeDtypeStruct((nt, 1, tm), jnp.int32),
                   jax.ShapeDtypeStruct((nt, 1, tm), jnp.int32),
                   jax.ShapeDtypeStruct((1, N_CLASSES), jnp.int32)],
        scratch_shapes=[pltpu.VMEM((1, N_CLASSES), jnp.float32)],
        compiler_params=_cparams(("arbitrary",)),
        name="route",
    )(logits, tri)


ROW_DMA_UNROLL = 8


def _dest_row(pstart_ref, cls_ref, rank_ref, r):
    return pstart_ref[cls_ref[0, 0, r]] + rank_ref[0, 0, r]


def _for_rows(tm, fn):
    def trip(g, c):
        for u in range(ROW_DMA_UNROLL):
            fn(g * ROW_DMA_UNROLL + u, u)
        return c

    lax.fori_loop(0, tm // ROW_DMA_UNROLL, trip, 0)


def _dispatch_kernel(tm, pstart_ref, cls_ref, rank_ref, pcls_ref, prank_ref, h2_ref, w_ref, rows_in, rows_out,
                     tile_ref, sems):
    del rows_in
    i = pl.program_id(0)
    slot = i % 2

    def row_copy(c_ref, r_ref, s, r):
        dest = _dest_row(pstart_ref, c_ref, r_ref, r)
        return pltpu.make_async_copy(tile_ref.at[s, pl.ds(r, 1)], rows_out.at[pl.ds(dest, 1)], sems.at[s])

    tile_ref[slot, :, 0:D_MODEL] = h2_ref[...]
    tile_ref[slot, :, D_MODEL:ROW_W] = w_ref[...]
    _for_rows(tm, lambda r, u: row_copy(cls_ref, rank_ref, slot, r).start(priority=u % 2))

    @pl.when(i > 0)
    def _():
        _for_rows(tm, lambda r, u: row_copy(pcls_ref, prank_ref, 1 - slot, r).wait())

    @pl.when(i == pl.num_programs(0) - 1)
    def _():
        _for_rows(tm, lambda r, u: row_copy(cls_ref, rank_ref, slot, r).wait())


def _dispatch(pstarts, cls3, rank3, h2, wslab, rows_zero, tm):
    n_tok = h2.shape[0]
    idx = lambda f: pl.BlockSpec((1, 1, tm), f, memory_space=pltpu.SMEM)
    grid_spec = pltpu.PrefetchScalarGridSpec(
        num_scalar_prefetch=1,
        grid=(n_tok // tm,),
        in_specs=[idx(lambda i, ps: (i, 0, 0)), idx(lambda i, ps: (i, 0, 0)),
                  idx(lambda i, ps: (jnp.maximum(i - 1, 0), 0, 0)), idx(lambda i, ps: (jnp.maximum(i - 1, 0), 0, 0)),
                  pl.BlockSpec((tm, D_MODEL), lambda i, ps: (i, 0)),
                  pl.BlockSpec((tm, LANES), lambda i, ps: (i, 0)),
                  pl.BlockSpec(memory_space=pl.ANY)],
        out_specs=pl.BlockSpec(memory_space=pl.ANY),
        scratch_shapes=[pltpu.VMEM((2, tm, ROW_W), jnp.float32), pltpu.SemaphoreType.DMA((2,))],
    )
    return pl.pallas_call(
        functools.partial(_dispatch_kernel, tm),
        grid_spec=grid_spec,
        out_shape=jax.ShapeDtypeStruct(rows_zero.shape, jnp.float32),
        input_output_aliases={7: 0},
        compiler_params=_cparams(("arbitrary",)),
        name="dispatch",
    )(pstarts, cls3, rank3, cls3, rank3, h2, wslab, rows_zero)


def _combine_kernel(tm, pstart_ref, cls_ref, rank_ref, ncls_ref, nrank_ref, x1_ref, y_ref, o_ref, buf_ref, sems):
    i = pl.program_id(0)
    slot = i % 2

    def row_copy(c_ref, r_ref, s, r):
        src = _dest_row(pstart_ref, c_ref, r_ref, r)
        return pltpu.make_async_copy(y_ref.at[pl.ds(src, 1)], buf_ref.at[s, pl.ds(r, 1)], sems.at[s])

    @pl.when(i == 0)
    def _():
        _for_rows(tm, lambda r, u: row_copy(cls_ref, rank_ref, slot, r).start(priority=u % 2))

    @pl.when(i < pl.num_programs(0) - 1)
    def _():
        _for_rows(tm, lambda r, u: row_copy(ncls_ref, nrank_ref, 1 - slot, r).start(priority=u % 2))

    _for_rows(tm, lambda r, u: row_copy(cls_ref, rank_ref, slot, r).wait())
    o_ref[...] = x1_ref[...] + buf_ref[slot]


def _combine(pstarts, cls3, rank3, x1, y, tm):
    n_tok = x1.shape[0]
    nt = n_tok // tm
    idx = lambda f: pl.BlockSpec((1, 1, tm), f, memory_space=pltpu.SMEM)
    grid_spec = pltpu.PrefetchScalarGridSpec(
        num_scalar_prefetch=1,
        grid=(nt,),
        in_specs=[idx(lambda i, ps: (i, 0, 0)), idx(lambda i, ps: (i, 0, 0)),
                  idx(lambda i, ps: (jnp.minimum(i + 1, nt - 1), 0, 0)),
                  idx(lambda i, ps: (jnp.minimum(i + 1, nt - 1), 0, 0)),
                  pl.BlockSpec((tm, D_MODEL), lambda i, ps: (i, 0)),
                  pl.BlockSpec(memory_space=pl.ANY)],
        out_specs=pl.BlockSpec((tm, D_MODEL), lambda i, ps: (i, 0)),
        scratch_shapes=[pltpu.VMEM((2, tm, D_MODEL), jnp.float32), pltpu.SemaphoreType.DMA((2,))],
    )
    return pl.pallas_call(
        functools.partial(_combine_kernel, tm),
        grid_spec=grid_spec,
        out_shape=jax.ShapeDtypeStruct(x1.shape, jnp.float32),
        compiler_params=_cparams(("arbitrary",)),
        name="combine",
    )(pstarts, cls3, rank3, cls3, rank3, x1, y)


def _expert_kernel(ea_ref, eb_ref, nused_ref, rows_ref, ua_ref, ub_ref, da_ref, db_ref, y_ref):
    b = pl.program_id(0)

    @pl.when(b < nused_ref[0])
    def _():
        h = rows_ref[:, 0:D_MODEL].astype(jnp.bfloat16)

        def expert(up_ref, down_ref, w):
            u = _dot(h, up_ref[0])
            act = (jax.nn.silu(u[:, 0:EXPERT_FF]) * u[:, EXPERT_FF:]).astype(jnp.bfloat16)
            return w * _dot(act, down_ref[0])

        y_ref[...] = (expert(ua_ref, da_ref, rows_ref[:, D_MODEL:D_MODEL + 1])
                      + expert(ub_ref, db_ref, rows_ref[:, D_MODEL + 1:D_MODEL + 2]))

    @pl.when(b >= nused_ref[0])
    def _():
        y_ref[...] = jnp.zeros_like(y_ref)


def _experts(blk_ea, blk_eb, nused, rows, w_up, w_down):
    n_blocks = rows.shape[0] // EXPERT_BLOCK
    grid_spec = pltpu.PrefetchScalarGridSpec(
        num_scalar_prefetch=3,
        grid=(n_blocks,),
        in_specs=[
            pl.BlockSpec((EXPERT_BLOCK, ROW_W), lambda b, ea, eb, nu: (b, 0)),
            pl.BlockSpec((1, D_MODEL, 2 * EXPERT_FF), lambda b, ea, eb, nu: (ea[b], 0, 0)),
            pl.BlockSpec((1, D_MODEL, 2 * EXPERT_FF), lambda b, ea, eb, nu: (eb[b], 0, 0)),
            pl.BlockSpec((1, EXPERT_FF, D_MODEL), lambda b, ea, eb, nu: (ea[b], 0, 0)),
            pl.BlockSpec((1, EXPERT_FF, D_MODEL), lambda b, ea, eb, nu: (eb[b], 0, 0)),
        ],
        out_specs=pl.BlockSpec((EXPERT_BLOCK, D_MODEL), lambda b, ea, eb, nu: (b, 0)),
    )
    return pl.pallas_call(
        _expert_kernel,
        grid_spec=grid_spec,
        out_shape=jax.ShapeDtypeStruct((rows.shape[0], D_MODEL), jnp.float32),
        compiler_params=_cparams(("arbitrary",)),
        name="experts",
    )(blk_ea, blk_eb, nused, rows, w_up, w_up, w_down, w_down)


def _rope_tables(positions):
    half = ROPE_DIM // 2
    inv = ROPE_THETA ** (-jnp.arange(0, ROPE_DIM, 2, dtype=jnp.float32) / ROPE_DIM)
    ang = positions.reshape(-1).astype(jnp.float32)[:, None] * inv
    cos, sin = jnp.cos(ang), jnp.sin(ang)
    ones = jnp.ones((ang.shape[0], HEAD_DIM - ROPE_DIM), jnp.float32)
    zeros = jnp.zeros_like(ones)
    zh = jnp.zeros_like(sin)
    c = jnp.concatenate([cos, cos, ones], axis=1)
    a = jnp.concatenate([-sin, zh, zeros], axis=1)
    b = jnp.concatenate([zh, sin, zeros], axis=1)
    rep = LANES // HEAD_DIM
    return jnp.tile(c, (1, rep)), jnp.tile(a, (1, rep)), jnp.tile(b, (1, rep))


def _pick_tile(n, pref):
    t = min(n, pref)
    while n % t:
        t //= 2
    return t


def kernel(x, mem, positions, attn_norm_g, w_in, diff_qnorm_g, diff_knorm_g, diff_lambda, diff_subln_g,
           fox_qnorm_g, fox_knorm_g, fox_forget_b, mem_norm_g, w_mem_kv, mem_qnorm_g, mem_knorm_g, w_o_diff,
           w_o_fox, w_o_mem, w_out, ffn_norm_g, w_router_group, b_router_group, w_router_expert,
           b_router_expert, w_up, w_down):
    n_batch, seq, _ = x.shape
    mem_len = mem.shape[1]
    n_tok = n_batch * seq
    bf = jnp.bfloat16
    tm = _pick_tile(seq, 512)
    tq = _pick_tile(seq, 512)
    tk = _pick_tile(tq, 512)
    tiles_per_seq = seq // tm
    x2 = x.reshape(n_tok, D_MODEL)

    wi = w_in[0]
    o = 0
    parts = {}
    for name, width in (("qa", HW), ("ka", HW), ("va", HW), ("qb", HW), ("kb", HW), ("vb", HW),
                        ("fb", FOX_HEADS), ("qc", HW), ("g", 3 * D_MODEL)):
        parts[name] = wi[:, o:o + width]
        o += width
    w_main = jnp.concatenate([parts[k] for k in ("qa", "ka", "va", "qb", "kb", "vb", "qc", "g")], axis=1).astype(bf)
    w_f = jnp.pad(parts["fb"], ((0, 0), (0, LANES - FOX_HEADS))).astype(bf)
    fb = jnp.pad(fox_forget_b[0], (0, LANES - FOX_HEADS)).reshape(1, LANES)
    tile_gain = lambda g, n: jnp.tile(g, n).reshape(1, -1)
    gains = (tile_gain(diff_qnorm_g[0], _NORM_CHUNK // HEAD_DIM), tile_gain(diff_knorm_g[0], _NORM_CHUNK // HEAD_DIM),
             tile_gain(fox_qnorm_g[0], _NORM_CHUNK // HEAD_DIM), tile_gain(fox_knorm_g[0], _NORM_CHUNK // HEAD_DIM),
             mem_qnorm_g[0].reshape(1, MEM_DIM))
    rope_c, rope_a, rope_b = _rope_tables(positions)

    kmem, vmem = _mem_kv(mem.reshape(n_batch * mem_len, D_MODEL), mem_norm_g[0].reshape(1, D_MODEL),
                         w_mem_kv[0].astype(bf), mem_knorm_g[0].reshape(1, MEM_DIM), n_batch, mem_len)

    qa, ka, va, qb, kb, vb, qc, gate, cum = _in_proj(
        x2, attn_norm_g[0].reshape(1, D_MODEL), w_main, w_f, fb, rope_c, rope_a, rope_b, gains, tm, tiles_per_seq)

    cum_t = cum[:, :FOX_HEADS].reshape(n_batch, seq, FOX_HEADS // 2, 2).transpose(0, 2, 3, 1)
    yb = _fox_attn(qb, kb, vb, cum_t, n_batch, seq, tq, tk)
    ya = _diff_attn(qa, ka, va, diff_lambda[0], diff_subln_g[0].reshape(1, LANES), n_batch, seq, tq, tk)

    w_r = jnp.concatenate([w_router_group[0], w_router_expert[0]], axis=1)
    w_r = jnp.pad(w_r, ((0, 0), (0, LANES - w_r.shape[1])))
    r_hi = w_r.astype(bf)
    r_lo = (w_r - r_hi.astype(jnp.float32)).astype(bf)
    r_b = jnp.pad(jnp.concatenate([b_router_group[0], b_router_expert[0]]), (0, LANES - N_GROUPS - N_EXPERTS))
    x1, h2, logits = _merge(x2, ya, yb, qc, kmem, vmem, gate, w_o_diff[0].astype(bf), w_o_fox[0].astype(bf),
                            w_o_mem[0].astype(bf), w_out[0].astype(bf), ffn_norm_g[0].reshape(1, D_MODEL),
                            r_hi, r_lo, r_b.reshape(1, LANES), tm, tiles_per_seq, mem_len)

    wslab, cls3, rank3, counts = _route(logits, tm)

    blk = EXPERT_BLOCK
    n_blocks = -(-(n_tok + N_LIVE_CLASSES * (blk - 1)) // blk)
    counts = counts[0]
    padded = (counts + blk - 1) // blk * blk
    pends = jnp.cumsum(padded)
    pstarts = pends - padded
    blk_cls = jnp.minimum(jnp.searchsorted(pends, jnp.arange(n_blocks, dtype=jnp.int32) * blk, side="right"),
                          N_CLASSES - 1).astype(jnp.int32)
    per_group = EXPERTS_PER_GROUP * EXPERTS_PER_GROUP
    blk_ea = blk_cls // EXPERTS_PER_GROUP
    blk_eb = (blk_cls // per_group) * EXPERTS_PER_GROUP + blk_cls % EXPERTS_PER_GROUP
    nused = (pends[-1] // blk).astype(jnp.int32).reshape(1)
    pstarts = pstarts.astype(jnp.int32)

    rows = _dispatch(pstarts, cls3, rank3, h2, wslab, jnp.zeros((n_blocks * blk, ROW_W), jnp.float32), tm)
    y = _experts(blk_ea, blk_eb, nused, rows, w_up[0].astype(bf), w_down[0].astype(bf))
    out = _combine(pstarts, cls3, rank3, x1, y, tm)
    return out.reshape(x.shape)
```

```python
import functools
import math

import jax
import jax.numpy as jnp
from jax import lax
from jax.experimental import pallas as pl
from jax.experimental.pallas import tpu as pltpu

D_MODEL = 1024
HEAD_DIM = 64
DIFF_HEADS = 4
FOX_HEADS = 8
MEM_HEADS = 4
MEM_DIM = 128
ROPE_THETA = 500000.0
ROPE_DIM = HEAD_DIM // 4
N_GROUPS = 4
EXPERTS_PER_GROUP = 8
N_EXPERTS = N_GROUPS * EXPERTS_PER_GROUP
EXPERT_FF = 512
EPS = 1e-6
NEG_INF = -1e30
LAMBDA_INIT = 0.8 - 0.6 * math.exp(-0.3 * 0)
LOG2E = 1.4426950408889634

LANES = 128
VMEM_LIMIT_BYTES = 56 * 1024 * 1024

HW = 512
N_CLASSES = 256
N_LIVE_CLASSES = N_GROUPS * (EXPERTS_PER_GROUP * (EXPERTS_PER_GROUP - 1) // 2)
ROW_W = D_MODEL + LANES
EXPERT_BLOCK = 256


def _cparams(sem):
    return pltpu.CompilerParams(dimension_semantics=sem, vmem_limit_bytes=VMEM_LIMIT_BYTES)


def _rms_rows(x, g):
    return x * lax.rsqrt(jnp.mean(x * x, axis=-1, keepdims=True) + EPS) * g


def _dot(a, b):
    return jnp.dot(a, b, preferred_element_type=jnp.float32)


def _dot_nt(a, b):
    return lax.dot_general(a, b, (((1,), (1,)), ((), ())), preferred_element_type=jnp.float32)


def _mem_kv_kernel(mem_ref, g_ref, w_ref, kg_ref, k_out, v_out):
    mn = _rms_rows(mem_ref[...], g_ref[...]).astype(jnp.bfloat16)
    kv = _dot(mn, w_ref[...])
    for h in range(MEM_HEADS):
        kh = kv[:, h * MEM_DIM:(h + 1) * MEM_DIM]
        k_out[:, h * MEM_DIM:(h + 1) * MEM_DIM] = _rms_rows(kh, kg_ref[...]).astype(jnp.bfloat16)
    v_out[...] = kv[:, HW:].astype(jnp.bfloat16)


def _mem_kv(mem2, g, w_kv, kg, n_batch, mem_len):
    return pl.pallas_call(
        _mem_kv_kernel,
        grid=(n_batch,),
        in_specs=[
            pl.BlockSpec((mem_len, D_MODEL), lambda b: (b, 0)),
            pl.BlockSpec((1, D_MODEL), lambda b: (0, 0)),
            pl.BlockSpec((D_MODEL, 2 * HW), lambda b: (0, 0)),
            pl.BlockSpec((1, MEM_DIM), lambda b: (0, 0)),
        ],
        out_specs=[
            pl.BlockSpec((mem_len, HW), lambda b: (b, 0)),
            pl.BlockSpec((mem_len, HW), lambda b: (b, 0)),
        ],
        out_shape=[jax.ShapeDtypeStruct((n_batch * mem_len, HW), jnp.bfloat16)] * 2,
        compiler_params=_cparams(("arbitrary",)),
        name="mem_kv",
    )(mem2, g, w_kv, kg)


_C_QA, _C_KA, _C_VA, _C_QB, _C_KB, _C_VB, _C_QC, _C_G = (i * HW for i in range(8))
_MAIN_COLS = 7 * HW + 3 * D_MODEL
_NORM_CHUNK = 256
_HEADS_PER_CHUNK = _NORM_CHUNK // HEAD_DIM
FOX_SLAB = FOX_HEADS * LANES
BIAS_PARTS = 3


def _group_rms(r, bd, g):
    ss = _dot((r * r).astype(jnp.bfloat16), bd)
    return r * lax.rsqrt(ss * (1.0 / HEAD_DIM) + EPS) * g


def _in_proj_kernel(tiles_per_seq, x_ref, g_ref, w_ref, wf_ref, fb_ref, rc_ref, ra_ref, rb_ref, bd_ref, tri_ref,
                    place_ref, gqa_ref, gka_ref, gqb_ref, gkb_ref, gqc_ref,
                    qa_o, ka_o, va_o, qb_o, kb_o, vb_o, qc_o, gate_o, cum_o, carry_ref):
    i = pl.program_id(0)

    @pl.when(i == 0)
    def _():
        carry_ref[...] = jnp.zeros_like(carry_ref)

    h = _rms_rows(x_ref[...], g_ref[...]).astype(jnp.bfloat16)
    bd = bd_ref[...]
    nchunk = HW // _NORM_CHUNK
    lane = lax.broadcasted_iota(jnp.int32, (h.shape[0], LANES), 1)

    def split3(v):
        p1 = v.astype(jnp.bfloat16)
        r1 = v - p1.astype(jnp.float32)
        p2 = r1.astype(jnp.bfloat16)
        return p1, p2, (r1 - p2.astype(jnp.float32)).astype(jnp.bfloat16)

    rope_c = jnp.concatenate([rc_ref[...]] * (_NORM_CHUNK // LANES), axis=1)
    rope_a = jnp.concatenate([ra_ref[...]] * (_NORM_CHUNK // LANES), axis=1)
    rope_b = jnp.concatenate([rb_ref[...]] * (_NORM_CHUNK // LANES), axis=1)
    half = ROPE_DIM // 2

    def rope(y):
        return (y * rope_c + pltpu.roll(y, _NORM_CHUNK - half, axis=1) * rope_a
                + pltpu.roll(y, half, axis=1) * rope_b)

    def normed_chunk(r, c, gain, use_rope, scale):
        y = _group_rms(r[:, c * _NORM_CHUNK:(c + 1) * _NORM_CHUNK], bd, gain)
        if use_rope:
            y = rope(y)
        return y if scale is None else y * scale

    def packed_branch(col0, out, gain_ref, scale):
        gain = gain_ref[...]
        r = _dot(h, w_ref[:, col0:col0 + HW])
        for c in range(nchunk):
            out[:, c * _NORM_CHUNK:(c + 1) * _NORM_CHUNK] = normed_chunk(r, c, gain, True, scale).astype(jnp.bfloat16)

    def slab_branch(col0, out, gain_ref, scale, spare):
        gain = gain_ref[...]
        r = _dot(h, w_ref[:, col0:col0 + HW])
        for c in range(nchunk):
            y = normed_chunk(r, c, gain, False, scale)
            for hh in range(_HEADS_PER_CHUNK):
                src = y[:, (hh // 2) * LANES:(hh // 2 + 1) * LANES]
                if hh % 2:
                    src = pltpu.roll(src, HEAD_DIM, axis=1)
                head = c * _HEADS_PER_CHUNK + hh
                out[:, head * LANES:(head + 1) * LANES] = jnp.where(lane < HEAD_DIM, src, spare(head)).astype(jnp.bfloat16)

    q_scale = HEAD_DIM ** -0.5 * LOG2E
    packed_branch(_C_QA, qa_o, gqa_ref, q_scale)

    z = _dot(h, wf_ref[...]) + fb_ref[...]
    logf = -(jnp.maximum(-z, 0.0) + jnp.log1p(jnp.exp(-jnp.abs(z))))
    packed_branch(_C_KA, ka_o, gka_ref, None)

    tri = tri_ref[...]
    carry = jnp.where(i % tiles_per_seq == 0, 0.0, carry_ref[...])
    cum = sum(_dot(tri, p) for p in split3(logf)) + carry
    cum_o[...] = cum
    carry_ref[...] = cum[cum.shape[0] - 1:, :]
    slab_branch(_C_QB, qb_o, gqb_ref, q_scale, lambda head: jnp.where(
        (lane >= HEAD_DIM) & (lane < HEAD_DIM + BIAS_PARTS), 1.0, 0.0))

    pieces = [p.astype(jnp.float32) for p in split3(jnp.where(lane < FOX_HEADS, cum * (-LOG2E), 0.0))]
    packed = pieces[0]
    for k in range(1, BIAS_PARTS):
        packed = packed + pltpu.roll(pieces[k], k * FOX_HEADS, axis=1)
    kbias = _dot(packed.astype(jnp.bfloat16), place_ref[...])
    slab_branch(_C_KB, kb_o, gkb_ref, None, lambda head: kbias[:, head * LANES:(head + 1) * LANES])

    va_o[...] = _dot(h, w_ref[:, _C_VA:_C_VA + HW]).astype(jnp.bfloat16)
    vb_o[...] = _dot(h, w_ref[:, _C_VB:_C_VB + HW]).astype(jnp.bfloat16)

    gqc = gqc_ref[...]
    rc = _dot(h, w_ref[:, _C_QC:_C_QC + HW])
    for hd in range(MEM_HEADS):
        sl = slice(hd * MEM_DIM, (hd + 1) * MEM_DIM)
        qc_o[:, sl] = (_rms_rows(rc[:, sl], gqc) * MEM_DIM ** -0.5).astype(jnp.bfloat16)

    for c in range(3 * D_MODEL // HW):
        lo = _C_G + c * HW
        r = _dot(h, w_ref[:, lo:lo + HW])
        gate_o[:, c * HW:(c + 1) * HW] = jax.nn.sigmoid(r).astype(jnp.bfloat16)


def _in_proj(x2, g, w_main, w_f, fb, rope_c, rope_a, rope_b, gains, tm, tiles_per_seq):
    n_tok = x2.shape[0]
    bd = (lax.broadcasted_iota(jnp.int32, (_NORM_CHUNK, _NORM_CHUNK), 0) // HEAD_DIM
          == lax.broadcasted_iota(jnp.int32, (_NORM_CHUNK, _NORM_CHUNK), 1) // HEAD_DIM).astype(jnp.bfloat16)
    tri = (lax.broadcasted_iota(jnp.int32, (tm, tm), 1)
           <= lax.broadcasted_iota(jnp.int32, (tm, tm), 0)).astype(jnp.bfloat16)
    src = lax.broadcasted_iota(jnp.int32, (LANES, FOX_SLAB), 0)
    dst = lax.broadcasted_iota(jnp.int32, (LANES, FOX_SLAB), 1)
    place = ((src < BIAS_PARTS * FOX_HEADS)
             & (dst == (src % FOX_HEADS) * LANES + HEAD_DIM + src // FOX_HEADS)).astype(jnp.bfloat16)
    const = lambda shape: pl.BlockSpec(shape, lambda i: (0, 0))
    resident = lambda shape: pl.BlockSpec(shape, lambda i: (0, 0), pipeline_mode=pl.Buffered(1))
    row = lambda w: pl.BlockSpec((tm, w), lambda i: (i, 0))
    slab = lambda w: jax.ShapeDtypeStruct((n_tok, w), jnp.bfloat16)
    return pl.pallas_call(
        functools.partial(_in_proj_kernel, tiles_per_seq),
        grid=(n_tok // tm,),
        in_specs=[row(D_MODEL), const((1, D_MODEL)), resident((D_MODEL, _MAIN_COLS)), const((D_MODEL, LANES)),
                  const((1, LANES)), row(LANES), row(LANES), row(LANES),
                  const((_NORM_CHUNK, _NORM_CHUNK)), const((tm, tm)), const((LANES, FOX_SLAB)),
                  const((1, _NORM_CHUNK)), const((1, _NORM_CHUNK)), const((1, _NORM_CHUNK)),
                  const((1, _NORM_CHUNK)), const((1, MEM_DIM))],
        out_specs=[row(HW), row(HW), row(HW), row(FOX_SLAB), row(FOX_SLAB), row(HW), row(HW),
                   row(3 * D_MODEL), row(LANES)],
        out_shape=[slab(HW), slab(HW), slab(HW), slab(FOX_SLAB), slab(FOX_SLAB), slab(HW), slab(HW),
                   slab(3 * D_MODEL), jax.ShapeDtypeStruct((n_tok, LANES), jnp.float32)],
        scratch_shapes=[pltpu.VMEM((1, LANES), jnp.float32)],
        compiler_params=_cparams(("arbitrary",)),
        name="in_proj",
    )(x2, g, w_main, w_f, fb, rope_c, rope_a, rope_b, bd, tri, place, *gains)


FOX_V_ROWS = 80
DIFF_V_ROWS = 144


Q_SPLIT = 2


def _flash_transposed(qs, k_of, v_of, shifts, tq, v_rows, finish):
    qi = pl.program_id(2)
    tqp = tq // Q_SPLIT
    parts = [(w, h) for w in range(len(qs)) for h in range(Q_SPLIT)]
    q_parts = [qs[w][h * tqp:(h + 1) * tqp] for (w, h) in parts]
    shift_parts = [0.0 if shifts[w] is None else shifts[w][:, h * tqp:(h + 1) * tqp] for (w, h) in parts]

    def block(k0, n, diag, state):
        def n_keys(h):
            return n if diag is None else min(n, diag + (h + 1) * tqp)

        sc = []
        for c, (w, h) in enumerate(parts):
            s = _dot_nt(k_of(w, k0, n_keys(h)), q_parts[c])
            if diag is not None:
                key = lax.broadcasted_iota(jnp.int32, s.shape, 0) - diag
                qry = lax.broadcasted_iota(jnp.int32, s.shape, 1) + h * tqp
                s = jnp.where(key <= qry, s, NEG_INF)
            sc.append((s, jnp.max(s, axis=0, keepdims=True)))
        out = []
        for c, (w, h) in enumerate(parts):
            (s, s_max), (m_old, acc_old), shift = sc[c], state[c], shift_parts[c]
            m_new = jnp.maximum(m_old, s_max + shift)
            p = jnp.exp2(s - (m_new - shift))
            acc_new = jnp.exp2(m_old - m_new) * acc_old + _dot(v_of(w, k0, n_keys(h)), p.astype(jnp.bfloat16))
            out.append((m_new, acc_new))
        return tuple(out)

    def done(state):
        finish([jnp.concatenate([state[w * Q_SPLIT + h][1] for h in range(Q_SPLIT)], axis=1)
                for w in range(len(qs))])

    init = tuple((jnp.full((1, tqp), NEG_INF, jnp.float32), jnp.zeros((v_rows, tqp), jnp.float32)) for _ in parts)
    state = lax.fori_loop(0, qi // 2, lambda j, st: block(pl.multiple_of(j * 2 * tq, 2 * tq), 2 * tq, None, st), init)

    @pl.when(qi % 2 == 1)
    def _():
        done(block(pl.multiple_of((qi - 1) * tq, tq), 2 * tq, tq, state))

    @pl.when(qi % 2 == 0)
    def _():
        done(block(pl.multiple_of(qi * tq, tq), tq, 0, state))


ATTN_CHAINS = 4


def _fill_values_t(v_ref, vt_ref, head_w, tk):
    n_heads, v_rows, seq = vt_ref.shape
    tail = lax.broadcasted_iota(jnp.int32, (v_rows - head_w, seq), 0)
    ones_row = jnp.where(tail == 0, 1.0, 0.0).astype(jnp.bfloat16)
    for h in range(n_heads):
        vt_ref[h, head_w:v_rows, :] = ones_row
    for c in range(seq // tk):
        blk_t = v_ref[c * tk:(c + 1) * tk, :].astype(jnp.float32).T
        for h in range(n_heads):
            vt_ref[h, 0:head_w, c * tk:(c + 1) * tk] = blk_t[h * head_w:(h + 1) * head_w].astype(jnp.bfloat16)

FOX_STEP_HEADS = ATTN_CHAINS
DIFF_STEP_HEADS = ATTN_CHAINS // 2


def _fox_kernel(tq, tk, q_ref, k_ref, v_ref, cq_ref, o_ref, vt_ref):
    heads = range(FOX_STEP_HEADS)

    @pl.when(pl.program_id(2) == 0)
    def _():
        _fill_values_t(v_ref, vt_ref, HEAD_DIM, tk)

    qs = [q_ref[:, w * LANES:(w + 1) * LANES] for w in heads]
    shifts = [cq_ref[0, w // 2, w % 2:w % 2 + 1, :] * LOG2E for w in heads]

    def finish(accs):
        o_t = jnp.concatenate([a[0:HEAD_DIM] / a[HEAD_DIM:HEAD_DIM + 1] for a in accs], axis=0)
        o_ref[...] = o_t.T.astype(jnp.bfloat16)

    _flash_transposed(qs, lambda w, k0, n: k_ref[pl.ds(k0, n), w * LANES:(w + 1) * LANES],
                      lambda w, k0, n: vt_ref[w, :, pl.ds(k0, n)], shifts, tq, FOX_V_ROWS, finish)


def _fox_attn(qb, kb, vb, cum_t, n_batch, seq, tq, tk):
    nq = seq // tq
    hs = FOX_STEP_HEADS
    return pl.pallas_call(
        functools.partial(_fox_kernel, tq, tk),
        grid=(n_batch, FOX_HEADS // hs, nq),
        in_specs=[
            pl.BlockSpec((tq, hs * LANES), lambda b, j, i: (b * nq + i, j)),
            pl.BlockSpec((seq, hs * LANES), lambda b, j, i: (b, j)),
            pl.BlockSpec((seq, hs * HEAD_DIM), lambda b, j, i: (b, j)),
            pl.BlockSpec((1, hs // 2, 2, tq), lambda b, j, i: (b, j, 0, i)),
        ],
        out_specs=pl.BlockSpec((tq, hs * HEAD_DIM), lambda b, j, i: (b * nq + i, j)),
        out_shape=jax.ShapeDtypeStruct((n_batch * seq, HW), jnp.bfloat16),
        scratch_shapes=[pltpu.VMEM((hs, FOX_V_ROWS, seq), jnp.bfloat16)],
        compiler_params=_cparams(("arbitrary", "arbitrary", "arbitrary")),
        name="fox_attn",
    )(qb, kb, vb, cum_t)


def _diff_kernel(tq, tk, q_ref, k_ref, v_ref, lam_ref, g_ref, o_ref, vt_ref):
    @pl.when(pl.program_id(2) == 0)
    def _():
        _fill_values_t(v_ref, vt_ref, 2 * HEAD_DIM, tk)

    lp = lam_ref[...]
    lam = (jnp.exp(jnp.sum(lp[0:1] * lp[1:2], axis=1, keepdims=True))
           - jnp.exp(jnp.sum(lp[2:3] * lp[3:4], axis=1, keepdims=True)) + LAMBDA_INIT)
    lane = lax.broadcasted_iota(jnp.int32, (tq, LANES), 1)
    qs = []
    for hd in range(DIFF_STEP_HEADS):
        q = q_ref[:, hd * LANES:(hd + 1) * LANES]
        zero = jnp.zeros_like(q)
        qs += [jnp.where(lane < HEAD_DIM, q, zero), jnp.where(lane >= HEAD_DIM, q, zero)]
    dv = 2 * HEAD_DIM

    def finish(accs):
        outs = []
        for hd in range(DIFF_STEP_HEADS):
            acc1, acc2 = accs[2 * hd], accs[2 * hd + 1]
            o_t = acc1[0:dv] / acc1[dv:dv + 1] - lam * (acc2[0:dv] / acc2[dv:dv + 1])
            outs.append(o_t * lax.rsqrt(jnp.mean(o_t * o_t, axis=0, keepdims=True) + EPS))
        gain = jnp.concatenate([g_ref[...]] * DIFF_STEP_HEADS, axis=1)
        o_ref[...] = (jnp.concatenate(outs, axis=0).T * gain * (1.0 - LAMBDA_INIT)).astype(jnp.bfloat16)

    _flash_transposed(qs, lambda w, k0, n: k_ref[pl.ds(k0, n), (w // 2) * LANES:(w // 2 + 1) * LANES],
                      lambda w, k0, n: vt_ref[w // 2, :, pl.ds(k0, n)], [None] * len(qs), tq, DIFF_V_ROWS, finish)


def _diff_attn(qa, ka, va, lam_params, subln_g, n_batch, seq, tq, tk):
    nq = seq // tq
    hs = DIFF_STEP_HEADS
    return pl.pallas_call(
        functools.partial(_diff_kernel, tq, tk),
        grid=(n_batch, DIFF_HEADS // hs, nq),
        in_specs=[
            pl.BlockSpec((tq, hs * LANES), lambda b, j, i: (b * nq + i, j)),
            pl.BlockSpec((seq, hs * LANES), lambda b, j, i: (b, j)),
            pl.BlockSpec((seq, hs * LANES), lambda b, j, i: (b, j)),
            pl.BlockSpec((4, HEAD_DIM), lambda b, j, i: (0, 0)),
            pl.BlockSpec((1, LANES), lambda b, j, i: (0, 0)),
        ],
        out_specs=pl.BlockSpec((tq, hs * LANES), lambda b, j, i: (b * nq + i, j)),
        out_shape=jax.ShapeDtypeStruct(qa.shape, jnp.bfloat16),
        scratch_shapes=[pltpu.VMEM((hs, DIFF_V_ROWS, seq), jnp.bfloat16)],
        compiler_params=_cparams(("arbitrary", "arbitrary", "arbitrary")),
        name="diff_attn",
    )(qa, ka, va, lam_params, subln_g)


def _merge_kernel(x_ref, ya_ref, yb_ref, qc_ref, km_ref, vm_ref, gate_ref, wd_ref, wf_ref, wm_ref, wo_ref,
                  fg_ref, rhi_ref, rlo_ref, rb_ref, tri_ref, x1_o, h2_o, w_o, cls_o, rank_o, cnt_o, carry_ref):
    @pl.when(pl.program_id(0) == 0)
    def _():
        carry_ref[...] = jnp.zeros_like(carry_ref)

    heads = []
    for h in range(MEM_HEADS):
        sl = slice(h * MEM_DIM, (h + 1) * MEM_DIM)
        s = _dot_nt(qc_ref[:, sl], km_ref[:, sl])
        p = jnp.exp(s - jnp.max(s, axis=1, keepdims=True))
        o = _dot(p.astype(jnp.bfloat16), vm_ref[:, sl]) / jnp.sum(p, axis=1, keepdims=True)
        heads.append(o.astype(jnp.bfloat16))
    yc = jnp.concatenate(heads, axis=1)

    merged = (gate_ref[:, 0:D_MODEL].astype(jnp.float32) * _dot(ya_ref[...], wd_ref[...])
              + gate_ref[:, D_MODEL:2 * D_MODEL].astype(jnp.float32) * _dot(yb_ref[...], wf_ref[...])
              + gate_ref[:, 2 * D_MODEL:3 * D_MODEL].astype(jnp.float32) * _dot(yc, wm_ref[...]))
    x1 = x_ref[...] + _dot(merged.astype(jnp.bfloat16), wo_ref[...])
    x1_o[...] = x1
    h2 = _rms_rows(x1, fg_ref[...])
    h2_o[...] = h2
    a_hi = h2.astype(jnp.bfloat16)
    a_lo = (h2 - a_hi.astype(jnp.float32)).astype(jnp.bfloat16)
    logits = (_dot(a_hi, rhi_ref[...]) + _dot(a_lo, rhi_ref[...]) + _dot(a_hi, rlo_ref[...])) + rb_ref[...]
    _route_tile(logits, tri_ref, w_o, cls_o, rank_o, cnt_o, carry_ref)


def _merge(x2, ya, yb, qc, kmem, vmem, gate, wd, wf, wm, wo, fg, rhi, rlo, rb, tm, tiles_per_seq, mem_len):
    n_tok = x2.shape[0]
    const = lambda shape: pl.BlockSpec(shape, lambda i: (0, 0))
    row = lambda w: pl.BlockSpec((tm, w), lambda i: (i, 0))
    memspec = pl.BlockSpec((mem_len, HW), lambda i: (i // tiles_per_seq, 0))
    nt = n_tok // tm
    tri = (lax.broadcasted_iota(jnp.int32, (tm, tm), 1)
           < lax.broadcasted_iota(jnp.int32, (tm, tm), 0)).astype(jnp.bfloat16)
    idx_row = pl.BlockSpec((1, 1, tm), lambda i: (i, 0, 0))
    return pl.pallas_call(
        _merge_kernel,
        grid=(nt,),
        in_specs=[row(D_MODEL), row(HW), row(HW), row(HW), memspec, memspec, row(3 * D_MODEL),
                  const((HW, D_MODEL)), const((HW, D_MODEL)), const((HW, D_MODEL)), const((D_MODEL, D_MODEL)),
                  const((1, D_MODEL)), const((D_MODEL, LANES)), const((D_MODEL, LANES)), const((1, LANES)),
                  const((tm, tm))],
        out_specs=[row(D_MODEL), row(D_MODEL), row(LANES), idx_row, idx_row, const((1, N_CLASSES))],
        out_shape=[jax.ShapeDtypeStruct((n_tok, D_MODEL), jnp.float32),
                   jax.ShapeDtypeStruct((n_tok, D_MODEL), jnp.float32),
                   jax.ShapeDtypeStruct((n_tok, LANES), jnp.float32),
                   jax.ShapeDtypeStruct((nt, 1, tm), jnp.int32),
                   jax.ShapeDtypeStruct((nt, 1, tm), jnp.int32),
                   jax.ShapeDtypeStruct((1, N_CLASSES), jnp.int32)],
        scratch_shapes=[pltpu.VMEM((1, N_CLASSES), jnp.float32)],
        compiler_params=_cparams(("arbitrary",)),
        name="merge",
    )(x2, ya, yb, qc, kmem, vmem, gate, wd, wf, wm, wo, fg, rhi, rlo, rb, tri)


_EXPERT_LANE0 = N_GROUPS


def _route_tile(lg, tri_ref, w_o, cls_o, rank_o, cnt_o, carry_ref):
    lanei = lax.broadcasted_iota(jnp.int32, lg.shape, 1)
    lane = lanei.astype(jnp.float32)
    ninf = jnp.float32(-jnp.inf)
    big = jnp.float32(LANES)

    def first_max(vals):
        top = jnp.max(vals, axis=1, keepdims=True)
        return top, jnp.min(jnp.where(vals == top, lane, big), axis=1, keepdims=True)

    is_group = lanei < N_GROUPS
    gmax, gidx = first_max(jnp.where(is_group, lg, ninf))
    gate = 1.0 / jnp.sum(jnp.where(is_group, jnp.exp(lg - gmax), 0.0), axis=1, keepdims=True)
    lo = _EXPERT_LANE0 + EXPERTS_PER_GROUP * gidx
    el = jnp.where((lane >= lo) & (lane < lo + EXPERTS_PER_GROUP), lg, ninf)
    t1, i1 = first_max(el)
    t2, i2 = first_max(jnp.where(lane == i1, ninf, el))
    e2 = jnp.exp(t2 - t1)
    w1 = gate * (1.0 / (1.0 + e2))
    w2 = gate * (e2 / (1.0 + e2))
    first_low = i1 < i2
    ea = jnp.minimum(i1, i2) - lo
    eb = jnp.maximum(i1, i2) - lo
    wa = jnp.where(first_low, w1, w2)
    wb = jnp.where(first_low, w2, w1)
    cls = (EXPERTS_PER_GROUP * EXPERTS_PER_GROUP) * gidx + EXPERTS_PER_GROUP * ea + eb

    cls_lane = lax.broadcasted_iota(jnp.int32, (lg.shape[0], N_CLASSES), 1).astype(jnp.float32)
    onehot = cls_lane == cls
    before = _dot(tri_ref[...], jnp.where(onehot, 1.0, 0.0).astype(jnp.bfloat16)) + carry_ref[...]
    rank = jnp.sum(jnp.where(onehot, before, 0.0), axis=1, keepdims=True)
    carry_ref[...] = carry_ref[...] + jnp.sum(jnp.where(onehot, 1.0, 0.0), axis=0, keepdims=True)
    cnt_o[...] = carry_ref[...].astype(jnp.int32)

    w_o[...] = jnp.where(lanei == 0, wa, jnp.where(lanei == 1, wb, 0.0))
    idx_t = jnp.where(lanei == 0, cls, jnp.where(lanei == 1, rank, 0.0)).T
    cls_o[0] = idx_t[0:1, :].astype(jnp.int32)
    rank_o[0] = idx_t[1:2, :].astype(jnp.int32)


ROW_DMA_UNROLL = 8


def _dest_row(pstart_ref, cls_ref, rank_ref, r):
    return pstart_ref[cls_ref[0, 0, r]] + rank_ref[0, 0, r]


def _for_rows(tm, fn):
    def trip(g, c):
        for u in range(ROW_DMA_UNROLL):
            fn(g * ROW_DMA_UNROLL + u, u)
        return c

    lax.fori_loop(0, tm // ROW_DMA_UNROLL, trip, 0)


def _dispatch_kernel(tm, pstart_ref, cls_ref, rank_ref, pcls_ref, prank_ref, h2_ref, w_ref, rows_in, rows_out,
                     tile_ref, sems):
    del rows_in
    i = pl.program_id(0)
    slot = i % 2

    def row_copy(c_ref, r_ref, s, r):
        dest = _dest_row(pstart_ref, c_ref, r_ref, r)
        return pltpu.make_async_copy(tile_ref.at[s, pl.ds(r, 1)], rows_out.at[pl.ds(dest, 1)], sems.at[s])

    tile_ref[slot, :, 0:D_MODEL] = h2_ref[...]
    tile_ref[slot, :, D_MODEL:ROW_W] = w_ref[...]
    _for_rows(tm, lambda r, u: row_copy(cls_ref, rank_ref, slot, r).start(priority=u % 2))

    @pl.when(i > 0)
    def _():
        _for_rows(tm, lambda r, u: row_copy(pcls_ref, prank_ref, 1 - slot, r).wait())

    @pl.when(i == pl.num_programs(0) - 1)
    def _():
        _for_rows(tm, lambda r, u: row_copy(cls_ref, rank_ref, slot, r).wait())


def _dispatch(pstarts, cls3, rank3, h2, wslab, rows_zero, tm):
    n_tok = h2.shape[0]
    idx = lambda f: pl.BlockSpec((1, 1, tm), f, memory_space=pltpu.SMEM)
    grid_spec = pltpu.PrefetchScalarGridSpec(
        num_scalar_prefetch=1,
        grid=(n_tok // tm,),
        in_specs=[idx(lambda i, ps: (i, 0, 0)), idx(lambda i, ps: (i, 0, 0)),
                  idx(lambda i, ps: (jnp.maximum(i - 1, 0), 0, 0)), idx(lambda i, ps: (jnp.maximum(i - 1, 0), 0, 0)),
                  pl.BlockSpec((tm, D_MODEL), lambda i, ps: (i, 0)),
                  pl.BlockSpec((tm, LANES), lambda i, ps: (i, 0)),
                  pl.BlockSpec(memory_space=pl.ANY)],
        out_specs=pl.BlockSpec(memory_space=pl.ANY),
        scratch_shapes=[pltpu.VMEM((2, tm, ROW_W), jnp.float32), pltpu.SemaphoreType.DMA((2,))],
    )
    return pl.pallas_call(
        functools.partial(_dispatch_kernel, tm),
        grid_spec=grid_spec,
        out_shape=jax.ShapeDtypeStruct(rows_zero.shape, jnp.float32),
        input_output_aliases={7: 0},
        compiler_params=_cparams(("arbitrary",)),
        name="dispatch",
    )(pstarts, cls3, rank3, cls3, rank3, h2, wslab, rows_zero)


def _combine_kernel(tm, pstart_ref, cls_ref, rank_ref, ncls_ref, nrank_ref, x1_ref, y_ref, o_ref, buf_ref, sems):
    i = pl.program_id(0)
    slot = i % 2

    def row_copy(c_ref, r_ref, s, r):
        src = _dest_row(pstart_ref, c_ref, r_ref, r)
        return pltpu.make_async_copy(y_ref.at[pl.ds(src, 1)], buf_ref.at[s, pl.ds(r, 1)], sems.at[s])

    @pl.when(i == 0)
    def _():
        _for_rows(tm, lambda r, u: row_copy(cls_ref, rank_ref, slot, r).start(priority=u % 2))

    @pl.when(i < pl.num_programs(0) - 1)
    def _():
        _for_rows(tm, lambda r, u: row_copy(ncls_ref, nrank_ref, 1 - slot, r).start(priority=u % 2))

    _for_rows(tm, lambda r, u: row_copy(cls_ref, rank_ref, slot, r).wait())
    o_ref[...] = x1_ref[...] + buf_ref[slot]


def _combine(pstarts, cls3, rank3, x1, y, tm):
    n_tok = x1.shape[0]
    nt = n_tok // tm
    idx = lambda f: pl.BlockSpec((1, 1, tm), f, memory_space=pltpu.SMEM)
    grid_spec = pltpu.PrefetchScalarGridSpec(
        num_scalar_prefetch=1,
        grid=(nt,),
        in_specs=[idx(lambda i, ps: (i, 0, 0)), idx(lambda i, ps: (i, 0, 0)),
                  idx(lambda i, ps: (jnp.minimum(i + 1, nt - 1), 0, 0)),
                  idx(lambda i, ps: (jnp.minimum(i + 1, nt - 1), 0, 0)),
                  pl.BlockSpec((tm, D_MODEL), lambda i, ps: (i, 0)),
                  pl.BlockSpec(memory_space=pl.ANY)],
        out_specs=pl.BlockSpec((tm, D_MODEL), lambda i, ps: (i, 0)),
        scratch_shapes=[pltpu.VMEM((2, tm, D_MODEL), jnp.float32), pltpu.SemaphoreType.DMA((2,))],
    )
    return pl.pallas_call(
        functools.partial(_combine_kernel, tm),
        grid_spec=grid_spec,
        out_shape=jax.ShapeDtypeStruct(x1.shape, jnp.float32),
        compiler_params=_cparams(("arbitrary",)),
        name="combine",
    )(pstarts, cls3, rank3, cls3, rank3, x1, y)


def _expert_kernel(ea_ref, eb_ref, nused_ref, rows_ref, ua_ref, ub_ref, da_ref, db_ref, y_ref):
    b = pl.program_id(0)

    @pl.when(b < nused_ref[0])
    def _():
        h = rows_ref[:, 0:D_MODEL].astype(jnp.bfloat16)

        def expert(up_ref, down_ref, w):
            u = _dot(h, up_ref[0])
            act = (jax.nn.silu(u[:, 0:EXPERT_FF]) * u[:, EXPERT_FF:]).astype(jnp.bfloat16)
            return w * _dot(act, down_ref[0])

        y_ref[...] = (expert(ua_ref, da_ref, rows_ref[:, D_MODEL:D_MODEL + 1])
                      + expert(ub_ref, db_ref, rows_ref[:, D_MODEL + 1:D_MODEL + 2]))

    @pl.when(b >= nused_ref[0])
    def _():
        y_ref[...] = jnp.zeros_like(y_ref)


def _experts(blk_ea, blk_eb, nused, rows, w_up, w_down):
    n_blocks = rows.shape[0] // EXPERT_BLOCK
    grid_spec = pltpu.PrefetchScalarGridSpec(
        num_scalar_prefetch=3,
        grid=(n_blocks,),
        in_specs=[
            pl.BlockSpec((EXPERT_BLOCK, ROW_W), lambda b, ea, eb, nu: (b, 0)),
            pl.BlockSpec((1, D_MODEL, 2 * EXPERT_FF), lambda b, ea, eb, nu: (ea[b], 0, 0)),
            pl.BlockSpec((1, D_MODEL, 2 * EXPERT_FF), lambda b, ea, eb, nu: (eb[b], 0, 0)),
            pl.BlockSpec((1, EXPERT_FF, D_MODEL), lambda b, ea, eb, nu: (ea[b], 0, 0)),
            pl.BlockSpec((1, EXPERT_FF, D_MODEL), lambda b, ea, eb, nu: (eb[b], 0, 0)),
        ],
        out_specs=pl.BlockSpec((EXPERT_BLOCK, D_MODEL), lambda b, ea, eb, nu: (b, 0)),
    )
    return pl.pallas_call(
        _expert_kernel,
        grid_spec=grid_spec,
        out_shape=jax.ShapeDtypeStruct((rows.shape[0], D_MODEL), jnp.float32),
        compiler_params=_cparams(("arbitrary",)),
        name="experts",
    )(blk_ea, blk_eb, nused, rows, w_up, w_up, w_down, w_down)


def _rope_tables(positions):
    half = ROPE_DIM // 2
    inv = ROPE_THETA ** (-jnp.arange(0, ROPE_DIM, 2, dtype=jnp.float32) / ROPE_DIM)
    ang = positions.reshape(-1).astype(jnp.float32)[:, None] * inv
    cos, sin = jnp.cos(ang), jnp.sin(ang)
    ones = jnp.ones((ang.shape[0], HEAD_DIM - ROPE_DIM), jnp.float32)
    zeros = jnp.zeros_like(ones)
    zh = jnp.zeros_like(sin)
    c = jnp.concatenate([cos, cos, ones], axis=1)
    a = jnp.concatenate([-sin, zh, zeros], axis=1)
    b = jnp.concatenate([zh, sin, zeros], axis=1)
    rep = LANES // HEAD_DIM
    return jnp.tile(c, (1, rep)), jnp.tile(a, (1, rep)), jnp.tile(b, (1, rep))


def _pick_tile(n, pref):
    t = min(n, pref)
    while n % t:
        t //= 2
    return t


def kernel(x, mem, positions, attn_norm_g, w_in, diff_qnorm_g, diff_knorm_g, diff_lambda, diff_subln_g,
           fox_qnorm_g, fox_knorm_g, fox_forget_b, mem_norm_g, w_mem_kv, mem_qnorm_g, mem_knorm_g, w_o_diff,
           w_o_fox, w_o_mem, w_out, ffn_norm_g, w_router_group, b_router_group, w_router_expert,
           b_router_expert, w_up, w_down):
    n_batch, seq, _ = x.shape
    mem_len = mem.shape[1]
    n_tok = n_batch * seq
    bf = jnp.bfloat16
    tm = _pick_tile(seq, 512)
    tq = _pick_tile(seq, 512)
    tk = _pick_tile(tq, 512)
    tiles_per_seq = seq // tm
    x2 = x.reshape(n_tok, D_MODEL)

    wi = w_in[0]
    o = 0
    parts = {}
    for name, width in (("qa", HW), ("ka", HW), ("va", HW), ("qb", HW), ("kb", HW), ("vb", HW),
                        ("fb", FOX_HEADS), ("qc", HW), ("g", 3 * D_MODEL)):
        parts[name] = wi[:, o:o + width]
        o += width
    w_main = jnp.concatenate([parts[k] for k in ("qa", "ka", "va", "qb", "kb", "vb", "qc", "g")], axis=1).astype(bf)
    w_f = jnp.pad(parts["fb"], ((0, 0), (0, LANES - FOX_HEADS))).astype(bf)
    fb = jnp.pad(fox_forget_b[0], (0, LANES - FOX_HEADS)).reshape(1, LANES)
    tile_gain = lambda g, n: jnp.tile(g, n).reshape(1, -1)
    gains = (tile_gain(diff_qnorm_g[0], _NORM_CHUNK // HEAD_DIM), tile_gain(diff_knorm_g[0], _NORM_CHUNK // HEAD_DIM),
             tile_gain(fox_qnorm_g[0], _NORM_CHUNK // HEAD_DIM), tile_gain(fox_knorm_g[0], _NORM_CHUNK // HEAD_DIM),
             mem_qnorm_g[0].reshape(1, MEM_DIM))
    rope_c, rope_a, rope_b = _rope_tables(positions)

    kmem, vmem = _mem_kv(mem.reshape(n_batch * mem_len, D_MODEL), mem_norm_g[0].reshape(1, D_MODEL),
                         w_mem_kv[0].astype(bf), mem_knorm_g[0].reshape(1, MEM_DIM), n_batch, mem_len)

    qa, ka, va, qb, kb, vb, qc, gate, cum = _in_proj(
        x2, attn_norm_g[0].reshape(1, D_MODEL), w_main, w_f, fb, rope_c, rope_a, rope_b, gains, tm, tiles_per_seq)

    cum_t = cum[:, :FOX_HEADS].reshape(n_batch, seq, FOX_HEADS // 2, 2).transpose(0, 2, 3, 1)
    yb = _fox_attn(qb, kb, vb, cum_t, n_batch, seq, tq, tk)
    ya = _diff_attn(qa, ka, va, diff_lambda[0], diff_subln_g[0].reshape(1, LANES), n_batch, seq, tq, tk)

    w_r = jnp.concatenate([w_router_group[0], w_router_expert[0]], axis=1)
    w_r = jnp.pad(w_r, ((0, 0), (0, LANES - w_r.shape[1])))
    r_hi = w_r.astype(bf)
    r_lo = (w_r - r_hi.astype(jnp.float32)).astype(bf)
    r_b = jnp.pad(jnp.concatenate([b_router_group[0], b_router_expert[0]]), (0, LANES - N_GROUPS - N_EXPERTS))
    x1, h2, wslab, cls3, rank3, counts = _merge(x2, ya, yb, qc, kmem, vmem, gate, w_o_diff[0].astype(bf), w_o_fox[0].astype(bf),
                            w_o_mem[0].astype(bf), w_out[0].astype(bf), ffn_norm_g[0].reshape(1, D_MODEL),
                            r_hi, r_lo, r_b.reshape(1, LANES), tm, tiles_per_seq, mem_len)

    blk = EXPERT_BLOCK
    n_blocks = -(-(n_tok + N_LIVE_CLASSES * (blk - 1)) // blk)
    counts = counts[0]
    padded = (counts + blk - 1) // blk * blk
    pends = jnp.cumsum(padded)
    pstarts = pends - padded
    blk_cls = jnp.minimum(jnp.searchsorted(pends, jnp.arange(n_blocks, dtype=jnp.int32) * blk, side="right"),
                          N_CLASSES - 1).astype(jnp.int32)
    per_group = EXPERTS_PER_GROUP * EXPERTS_PER_GROUP
    blk_ea = blk_cls // EXPERTS_PER_GROUP
    blk_eb = (blk_cls // per_group) * EXPERTS_PER_GROUP + blk_cls % EXPERTS_PER_GROUP
    nused = (pends[-1] // blk).astype(jnp.int32).reshape(1)
    pstarts = pstarts.astype(jnp.int32)

    rows = _dispatch(pstarts, cls3, rank3, h2, wslab, jnp.zeros((n_blocks * blk, ROW_W), jnp.float32), tm)
    y = _experts(blk_ea, blk_eb, nused, rows, w_up[0].astype(bf), w_down[0].astype(bf))
    out = _combine(pstarts, cls3, rank3, x1, y, tm)
    return out.reshape(x.shape)
```

```python
import functools
import math

import jax
import jax.numpy as jnp
from jax import lax
from jax.experimental import pallas as pl
from jax.experimental.pallas import tpu as pltpu

D_MODEL = 1024
HEAD_DIM = 64
DIFF_HEADS = 4
FOX_HEADS = 8
MEM_HEADS = 4
MEM_DIM = 128
ROPE_THETA = 500000.0
ROPE_DIM = HEAD_DIM // 4
N_GROUPS = 4
EXPERTS_PER_GROUP = 8
N_EXPERTS = N_GROUPS * EXPERTS_PER_GROUP
EXPERT_FF = 512
EPS = 1e-6
NEG_INF = -1e30
LAMBDA_INIT = 0.8 - 0.6 * math.exp(-0.3 * 0)
LOG2E = 1.4426950408889634

LANES = 128
VMEM_LIMIT_BYTES = 56 * 1024 * 1024

HW = 512
N_CLASSES = 256
N_LIVE_CLASSES = N_GROUPS * (EXPERTS_PER_GROUP * (EXPERTS_PER_GROUP - 1) // 2)
ROW_W = D_MODEL + LANES
EXPERT_BLOCK = 256


def _cparams(sem):
    return pltpu.CompilerParams(dimension_semantics=sem, vmem_limit_bytes=VMEM_LIMIT_BYTES)


def _rms_rows(x, g):
    return x * lax.rsqrt(jnp.mean(x * x, axis=-1, keepdims=True) + EPS) * g


def _dot(a, b):
    return jnp.dot(a, b, preferred_element_type=jnp.float32)


def _dot_nt(a, b):
    return lax.dot_general(a, b, (((1,), (1,)), ((), ())), preferred_element_type=jnp.float32)


def _mem_kv_kernel(mem_ref, g_ref, w_ref, kg_ref, k_out, v_out):
    mn = _rms_rows(mem_ref[...], g_ref[...]).astype(jnp.bfloat16)
    kv = _dot(mn, w_ref[...])
    for h in range(MEM_HEADS):
        kh = kv[:, h * MEM_DIM:(h + 1) * MEM_DIM]
        k_out[:, h * MEM_DIM:(h + 1) * MEM_DIM] = _rms_rows(kh, kg_ref[...]).astype(jnp.bfloat16)
    v_out[...] = kv[:, HW:].astype(jnp.bfloat16)


def _mem_kv(mem2, g, w_kv, kg, n_batch, mem_len):
    return pl.pallas_call(
        _mem_kv_kernel,
        grid=(n_batch,),
        in_specs=[
            pl.BlockSpec((mem_len, D_MODEL), lambda b: (b, 0)),
            pl.BlockSpec((1, D_MODEL), lambda b: (0, 0)),
            pl.BlockSpec((D_MODEL, 2 * HW), lambda b: (0, 0)),
            pl.BlockSpec((1, MEM_DIM), lambda b: (0, 0)),
        ],
        out_specs=[
            pl.BlockSpec((mem_len, HW), lambda b: (b, 0)),
            pl.BlockSpec((mem_len, HW), lambda b: (b, 0)),
        ],
        out_shape=[jax.ShapeDtypeStruct((n_batch * mem_len, HW), jnp.bfloat16)] * 2,
        compiler_params=_cparams(("arbitrary",)),
        name="mem_kv",
    )(mem2, g, w_kv, kg)


_C_QA, _C_KA, _C_VA, _C_QB, _C_KB, _C_VB, _C_QC, _C_G = (i * HW for i in range(8))
_MAIN_COLS = 7 * HW + 3 * D_MODEL
_NORM_CHUNK = 256
_HEADS_PER_CHUNK = _NORM_CHUNK // HEAD_DIM
FOX_SLAB = FOX_HEADS * LANES
BIAS_PARTS = 3


def _group_rms(r, bd, g):
    ss = _dot((r * r).astype(jnp.bfloat16), bd)
    return r * lax.rsqrt(ss * (1.0 / HEAD_DIM) + EPS) * g


def _in_proj_kernel(tiles_per_seq, x_ref, g_ref, w_ref, wf_ref, fb_ref, pos_ref, rinv_ref, ra_ref, rb_ref, bd_ref, tri_ref,
                    place_ref, gqa_ref, gka_ref, gqb_ref, gkb_ref, gqc_ref,
                    qa_o, ka_o, va_o, qb_o, kb_o, vb_o, qc_o, gate_o, cum_o, carry_ref):
    i = pl.program_id(0)

    @pl.when(i == 0)
    def _():
        carry_ref[...] = jnp.zeros_like(carry_ref)

    h = _rms_rows(x_ref[...], g_ref[...]).astype(jnp.bfloat16)
    bd = bd_ref[...]
    nchunk = HW // _NORM_CHUNK
    lane = lax.broadcasted_iota(jnp.int32, (h.shape[0], LANES), 1)

    def split3(v):
        p1 = v.astype(jnp.bfloat16)
        r1 = v - p1.astype(jnp.float32)
        p2 = r1.astype(jnp.bfloat16)
        return p1, p2, (r1 - p2.astype(jnp.float32)).astype(jnp.bfloat16)

    ang = pos_ref[...].astype(jnp.float32) * rinv_ref[...]
    sin = jnp.sin(ang)
    rep = _NORM_CHUNK // LANES
    rope_c = jnp.concatenate([jnp.cos(ang)] * rep, axis=1)
    rope_a = jnp.concatenate([sin * ra_ref[...]] * rep, axis=1)
    rope_b = jnp.concatenate([sin * rb_ref[...]] * rep, axis=1)
    half = ROPE_DIM // 2

    def rope(y):
        return (y * rope_c + pltpu.roll(y, _NORM_CHUNK - half, axis=1) * rope_a
                + pltpu.roll(y, half, axis=1) * rope_b)

    def normed_chunk(r, c, gain, use_rope, scale):
        y = _group_rms(r[:, c * _NORM_CHUNK:(c + 1) * _NORM_CHUNK], bd, gain)
        if use_rope:
            y = rope(y)
        return y if scale is None else y * scale

    def packed_branch(col0, out, gain_ref, scale):
        gain = gain_ref[...]
        r = _dot(h, w_ref[:, col0:col0 + HW])
        for c in range(nchunk):
            out[:, c * _NORM_CHUNK:(c + 1) * _NORM_CHUNK] = normed_chunk(r, c, gain, True, scale).astype(jnp.bfloat16)

    def slab_branch(col0, out, gain_ref, scale, spare):
        gain = gain_ref[...]
        r = _dot(h, w_ref[:, col0:col0 + HW])
        for c in range(nchunk):
            y = normed_chunk(r, c, gain, False, scale)
            for hh in range(_HEADS_PER_CHUNK):
                src = y[:, (hh // 2) * LANES:(hh // 2 + 1) * LANES]
                if hh % 2:
                    src = pltpu.roll(src, HEAD_DIM, axis=1)
                head = c * _HEADS_PER_CHUNK + hh
                out[:, head * LANES:(head + 1) * LANES] = jnp.where(lane < HEAD_DIM, src, spare(head)).astype(jnp.bfloat16)

    q_scale = HEAD_DIM ** -0.5 * LOG2E
    slab_branch(_C_QB, qb_o, gqb_ref, q_scale, lambda head: jnp.where(
        (lane >= HEAD_DIM) & (lane < HEAD_DIM + BIAS_PARTS), 1.0, 0.0))

    z = _dot(h, wf_ref[...]) + fb_ref[...]
    logf = -(jnp.maximum(-z, 0.0) + jnp.log1p(jnp.exp(-jnp.abs(z))))
    va_o[...] = _dot(h, w_ref[:, _C_VA:_C_VA + HW]).astype(jnp.bfloat16)
    vb_o[...] = _dot(h, w_ref[:, _C_VB:_C_VB + HW]).astype(jnp.bfloat16)

    tri = tri_ref[...]
    carry = jnp.where(i % tiles_per_seq == 0, 0.0, carry_ref[...])
    cum = sum(_dot(tri, p) for p in split3(logf)) + carry
    cum_o[...] = cum
    carry_ref[...] = cum[cum.shape[0] - 1:, :]

    gqc = gqc_ref[...]
    rc = _dot(h, w_ref[:, _C_QC:_C_QC + HW])
    for hd in range(MEM_HEADS):
        sl = slice(hd * MEM_DIM, (hd + 1) * MEM_DIM)
        qc_o[:, sl] = (_rms_rows(rc[:, sl], gqc) * MEM_DIM ** -0.5).astype(jnp.bfloat16)

    packed_branch(_C_QA, qa_o, gqa_ref, q_scale)

    pieces = [p.astype(jnp.float32) for p in split3(jnp.where(lane < FOX_HEADS, cum * (-LOG2E), 0.0))]
    packed = pieces[0]
    for k in range(1, BIAS_PARTS):
        packed = packed + pltpu.roll(pieces[k], k * FOX_HEADS, axis=1)
    kbias = _dot(packed.astype(jnp.bfloat16), place_ref[...])
    packed_branch(_C_KA, ka_o, gka_ref, None)
    slab_branch(_C_KB, kb_o, gkb_ref, None, lambda head: kbias[:, head * LANES:(head + 1) * LANES])

    for c in range(3 * D_MODEL // HW):
        lo = _C_G + c * HW
        r = _dot(h, w_ref[:, lo:lo + HW])
        gate_o[:, c * HW:(c + 1) * HW] = jax.nn.sigmoid(r).astype(jnp.bfloat16)


def _in_proj(x2, g, w_main, w_f, fb, pos, gains, tm, tiles_per_seq):
    n_tok = x2.shape[0]
    half = ROPE_DIM // 2
    head_lane = jnp.arange(LANES, dtype=jnp.int32) % HEAD_DIM
    inv = ROPE_THETA ** (-jnp.arange(0, ROPE_DIM, 2, dtype=jnp.float32) / ROPE_DIM)
    rope_inv = jnp.where(head_lane < ROPE_DIM, inv[head_lane % half], 0.0).reshape(1, LANES)
    rope_sa = jnp.where(head_lane < half, -1.0, 0.0).reshape(1, LANES)
    rope_sb = jnp.where((head_lane >= half) & (head_lane < ROPE_DIM), 1.0, 0.0).reshape(1, LANES)
    bd = (lax.broadcasted_iota(jnp.int32, (_NORM_CHUNK, _NORM_CHUNK), 0) // HEAD_DIM
          == lax.broadcasted_iota(jnp.int32, (_NORM_CHUNK, _NORM_CHUNK), 1) // HEAD_DIM).astype(jnp.bfloat16)
    tri = (lax.broadcasted_iota(jnp.int32, (tm, tm), 1)
           <= lax.broadcasted_iota(jnp.int32, (tm, tm), 0)).astype(jnp.bfloat16)
    src = lax.broadcasted_iota(jnp.int32, (LANES, FOX_SLAB), 0)
    dst = lax.broadcasted_iota(jnp.int32, (LANES, FOX_SLAB), 1)
    place = ((src < BIAS_PARTS * FOX_HEADS)
             & (dst == (src % FOX_HEADS) * LANES + HEAD_DIM + src // FOX_HEADS)).astype(jnp.bfloat16)
    const = lambda shape: pl.BlockSpec(shape, lambda i: (0, 0))
    resident = lambda shape: pl.BlockSpec(shape, lambda i: (0, 0), pipeline_mode=pl.Buffered(1))
    row = lambda w: pl.BlockSpec((tm, w), lambda i: (i, 0))
    slab = lambda w: jax.ShapeDtypeStruct((n_tok, w), jnp.bfloat16)
    return pl.pallas_call(
        functools.partial(_in_proj_kernel, tiles_per_seq),
        grid=(n_tok // tm,),
        in_specs=[row(D_MODEL), const((1, D_MODEL)), resident((D_MODEL, _MAIN_COLS)), const((D_MODEL, LANES)),
                  const((1, LANES)), row(1), const((1, LANES)), const((1, LANES)), const((1, LANES)),
                  const((_NORM_CHUNK, _NORM_CHUNK)), const((tm, tm)), const((LANES, FOX_SLAB)),
                  const((1, _NORM_CHUNK)), const((1, _NORM_CHUNK)), const((1, _NORM_CHUNK)),
                  const((1, _NORM_CHUNK)), const((1, MEM_DIM))],
        out_specs=[row(HW), row(HW), row(HW), row(FOX_SLAB), row(FOX_SLAB), row(HW), row(HW),
                   row(3 * D_MODEL), row(LANES)],
        out_shape=[slab(HW), slab(HW), slab(HW), slab(FOX_SLAB), slab(FOX_SLAB), slab(HW), slab(HW),
                   slab(3 * D_MODEL), jax.ShapeDtypeStruct((n_tok, LANES), jnp.float32)],
        scratch_shapes=[pltpu.VMEM((1, LANES), jnp.float32)],
        compiler_params=_cparams(("arbitrary",)),
        name="in_proj",
    )(x2, g, w_main, w_f, fb, pos, rope_inv, rope_sa, rope_sb, bd, tri, place, *gains)


FOX_V_ROWS = 80
DIFF_V_ROWS = 144


Q_SPLIT = 2


def _flash_transposed(qs, k_of, v_of, shifts, tq, v_rows, finish):
    qi = pl.program_id(2)
    tqp = tq // Q_SPLIT
    parts = [(w, h) for w in range(len(qs)) for h in range(Q_SPLIT)]
    q_parts = [qs[w][h * tqp:(h + 1) * tqp] for (w, h) in parts]
    shift_parts = [0.0 if shifts[w] is None else shifts[w][:, h * tqp:(h + 1) * tqp] for (w, h) in parts]

    def block(k0, n, diag, state):
        def n_keys(h):
            return n if diag is None else min(n, diag + (h + 1) * tqp)

        sc = []
        for c, (w, h) in enumerate(parts):
            s = _dot_nt(k_of(w, k0, n_keys(h)), q_parts[c])
            if diag is not None:
                key = lax.broadcasted_iota(jnp.int32, s.shape, 0) - diag
                qry = lax.broadcasted_iota(jnp.int32, s.shape, 1) + h * tqp
                s = jnp.where(key <= qry, s, NEG_INF)
            sc.append((s, jnp.max(s, axis=0, keepdims=True)))
        out = []
        for c, (w, h) in enumerate(parts):
            (s, s_max), (m_old, acc_old), shift = sc[c], state[c], shift_parts[c]
            m_new = jnp.maximum(m_old, s_max + shift)
            p = jnp.exp2(s - (m_new - shift))
            acc_new = jnp.exp2(m_old - m_new) * acc_old + _dot(v_of(w, k0, n_keys(h)), p.astype(jnp.bfloat16))
            out.append((m_new, acc_new))
        return tuple(out)

    def done(state):
        finish([jnp.concatenate([state[w * Q_SPLIT + h][1] for h in range(Q_SPLIT)], axis=1)
                for w in range(len(qs))])

    init = tuple((jnp.full((1, tqp), NEG_INF, jnp.float32), jnp.zeros((v_rows, tqp), jnp.float32)) for _ in parts)
    state = lax.fori_loop(0, qi // 2, lambda j, st: block(pl.multiple_of(j * 2 * tq, 2 * tq), 2 * tq, None, st), init)

    @pl.when(qi % 2 == 1)
    def _():
        done(block(pl.multiple_of((qi - 1) * tq, tq), 2 * tq, tq, state))

    @pl.when(qi % 2 == 0)
    def _():
        done(block(pl.multiple_of(qi * tq, tq), tq, 0, state))


ATTN_CHAINS = 4


def _fill_values_t(v_ref, vt_ref, head_w, tk):
    n_heads, v_rows, seq = vt_ref.shape
    tail = lax.broadcasted_iota(jnp.int32, (v_rows - head_w, seq), 0)
    ones_row = jnp.where(tail == 0, 1.0, 0.0).astype(jnp.bfloat16)
    for h in range(n_heads):
        vt_ref[h, head_w:v_rows, :] = ones_row
    for c in range(seq // tk):
        blk_t = v_ref[c * tk:(c + 1) * tk, :].astype(jnp.float32).T
        for h in range(n_heads):
            vt_ref[h, 0:head_w, c * tk:(c + 1) * tk] = blk_t[h * head_w:(h + 1) * head_w].astype(jnp.bfloat16)

FOX_STEP_HEADS = ATTN_CHAINS
DIFF_STEP_HEADS = ATTN_CHAINS // 2


def _fox_kernel(tq, tk, q_ref, k_ref, v_ref, cq_ref, o_ref, vt_ref):
    heads = range(FOX_STEP_HEADS)

    @pl.when(pl.program_id(2) == 0)
    def _():
        _fill_values_t(v_ref, vt_ref, HEAD_DIM, tk)

    qs = [q_ref[:, w * LANES:(w + 1) * LANES] for w in heads]
    shifts = [cq_ref[0, w // 2, w % 2:w % 2 + 1, :] * LOG2E for w in heads]

    def finish(accs):
        o_t = jnp.concatenate([a[0:HEAD_DIM] / a[HEAD_DIM:HEAD_DIM + 1] for a in accs], axis=0)
        o_ref[...] = o_t.T.astype(jnp.bfloat16)

    _flash_transposed(qs, lambda w, k0, n: k_ref[pl.ds(k0, n), w * LANES:(w + 1) * LANES],
                      lambda w, k0, n: vt_ref[w, :, pl.ds(k0, n)], shifts, tq, FOX_V_ROWS, finish)


def _fox_attn(qb, kb, vb, cum_t, n_batch, seq, tq, tk):
    nq = seq // tq
    hs = FOX_STEP_HEADS
    return pl.pallas_call(
        functools.partial(_fox_kernel, tq, tk),
        grid=(n_batch, FOX_HEADS // hs, nq),
        in_specs=[
            pl.BlockSpec((tq, hs * LANES), lambda b, j, i: (b * nq + i, j)),
            pl.BlockSpec((seq, hs * LANES), lambda b, j, i: (b, j)),
            pl.BlockSpec((seq, hs * HEAD_DIM), lambda b, j, i: (b, j)),
            pl.BlockSpec((1, hs // 2, 2, tq), lambda b, j, i: (b, j, 0, i)),
        ],
        out_specs=pl.BlockSpec((tq, hs * HEAD_DIM), lambda b, j, i: (b * nq + i, j)),
        out_shape=jax.ShapeDtypeStruct((n_batch * seq, HW), jnp.bfloat16),
        scratch_shapes=[pltpu.VMEM((hs, FOX_V_ROWS, seq), jnp.bfloat16)],
        compiler_params=_cparams(("arbitrary", "arbitrary", "arbitrary")),
        name="fox_attn",
    )(qb, kb, vb, cum_t)


def _diff_kernel(tq, tk, q_ref, k_ref, v_ref, lam_ref, g_ref, o_ref, vt_ref):
    @pl.when(pl.program_id(2) == 0)
    def _():
        _fill_values_t(v_ref, vt_ref, 2 * HEAD_DIM, tk)

    lp = lam_ref[...]
    lam = (jnp.exp(jnp.sum(lp[0:1] * lp[1:2], axis=1, keepdims=True))
           - jnp.exp(jnp.sum(lp[2:3] * lp[3:4], axis=1, keepdims=True)) + LAMBDA_INIT)
    lane = lax.broadcasted_iota(jnp.int32, (tq, LANES), 1)
    qs = []
    for hd in range(DIFF_STEP_HEADS):
        q = q_ref[:, hd * LANES:(hd + 1) * LANES]
        zero = jnp.zeros_like(q)
        qs += [jnp.where(lane < HEAD_DIM, q, zero), jnp.where(lane >= HEAD_DIM, q, zero)]
    dv = 2 * HEAD_DIM

    def finish(accs):
        outs = []
        for hd in range(DIFF_STEP_HEADS):
            acc1, acc2 = accs[2 * hd], accs[2 * hd + 1]
            o_t = acc1[0:dv] / acc1[dv:dv + 1] - lam * (acc2[0:dv] / acc2[dv:dv + 1])
            outs.append(o_t * lax.rsqrt(jnp.mean(o_t * o_t, axis=0, keepdims=True) + EPS))
        gain = jnp.concatenate([g_ref[...]] * DIFF_STEP_HEADS, axis=1)
        o_ref[...] = (jnp.concatenate(outs, axis=0).T * gain * (1.0 - LAMBDA_INIT)).astype(jnp.bfloat16)

    _flash_transposed(qs, lambda w, k0, n: k_ref[pl.ds(k0, n), (w // 2) * LANES:(w // 2 + 1) * LANES],
                      lambda w, k0, n: vt_ref[w // 2, :, pl.ds(k0, n)], [None] * len(qs), tq, DIFF_V_ROWS, finish)


def _diff_attn(qa, ka, va, lam_params, subln_g, n_batch, seq, tq, tk):
    nq = seq // tq
    hs = DIFF_STEP_HEADS
    return pl.pallas_call(
        functools.partial(_diff_kernel, tq, tk),
        grid=(n_batch, DIFF_HEADS // hs, nq),
        in_specs=[
            pl.BlockSpec((tq, hs * LANES), lambda b, j, i: (b * nq + i, j)),
            pl.BlockSpec((seq, hs * LANES), lambda b, j, i: (b, j)),
            pl.BlockSpec((seq, hs * LANES), lambda b, j, i: (b, j)),
            pl.BlockSpec((4, HEAD_DIM), lambda b, j, i: (0, 0)),
            pl.BlockSpec((1, LANES), lambda b, j, i: (0, 0)),
        ],
        out_specs=pl.BlockSpec((tq, hs * LANES), lambda b, j, i: (b * nq + i, j)),
        out_shape=jax.ShapeDtypeStruct(qa.shape, jnp.bfloat16),
        scratch_shapes=[pltpu.VMEM((hs, DIFF_V_ROWS, seq), jnp.bfloat16)],
        compiler_params=_cparams(("arbitrary", "arbitrary", "arbitrary")),
        name="diff_attn",
    )(qa, ka, va, lam_params, subln_g)


def _merge_kernel(x_ref, ya_ref, yb_ref, qc_ref, km_ref, vm_ref, gate_ref, wd_ref, wf_ref, wm_ref, wo_ref,
                  fg_ref, rhi_ref, rlo_ref, rb_ref, tri_ref, x1_o, h2_o, w_o, cls_o, rank_o, cnt_o, carry_ref, lg_ref):
    i = pl.program_id(0)

    @pl.when(i == 0)
    def _():
        carry_ref[...] = jnp.zeros_like(carry_ref)
        lg_ref[...] = jnp.zeros_like(lg_ref)

    lg_prev = lg_ref[(i + 1) % 2]

    heads = []
    for h in range(MEM_HEADS):
        sl = slice(h * MEM_DIM, (h + 1) * MEM_DIM)
        s = _dot_nt(qc_ref[:, sl], km_ref[:, sl])
        p = jnp.exp(s - jnp.max(s, axis=1, keepdims=True))
        o = _dot(p.astype(jnp.bfloat16), vm_ref[:, sl]) / jnp.sum(p, axis=1, keepdims=True)
        heads.append(o.astype(jnp.bfloat16))
    yc = jnp.concatenate(heads, axis=1)

    merged = (gate_ref[:, 0:D_MODEL].astype(jnp.float32) * _dot(ya_ref[...], wd_ref[...])
              + gate_ref[:, D_MODEL:2 * D_MODEL].astype(jnp.float32) * _dot(yb_ref[...], wf_ref[...])
              + gate_ref[:, 2 * D_MODEL:3 * D_MODEL].astype(jnp.float32) * _dot(yc, wm_ref[...]))
    _route_tile(lg_prev, i > 0, tri_ref, w_o, cls_o, rank_o, cnt_o, carry_ref)
    x1 = x_ref[...] + _dot(merged.astype(jnp.bfloat16), wo_ref[...])
    x1_o[...] = x1
    h2 = _rms_rows(x1, fg_ref[...])
    h2_o[...] = h2
    a_hi = h2.astype(jnp.bfloat16)
    a_lo = (h2 - a_hi.astype(jnp.float32)).astype(jnp.bfloat16)
    lg_ref[i % 2] = (_dot(a_hi, rhi_ref[...]) + _dot(a_lo, rhi_ref[...]) + _dot(a_hi, rlo_ref[...])) + rb_ref[...]


def _merge(x2, ya, yb, qc, kmem, vmem, gate, wd, wf, wm, wo, fg, rhi, rlo, rb, tm, tiles_per_seq, mem_len):
    n_tok = x2.shape[0]
    nt = n_tok // tm
    cur = lambda i: jnp.minimum(i, nt - 1)
    prev = lambda i: jnp.maximum(i - 1, 0)
    const = lambda shape: pl.BlockSpec(shape, lambda i: (0, 0))
    row = lambda w: pl.BlockSpec((tm, w), lambda i: (cur(i), 0))
    memspec = pl.BlockSpec((mem_len, HW), lambda i: (cur(i) // tiles_per_seq, 0))
    tri = (lax.broadcasted_iota(jnp.int32, (tm, tm), 1)
           < lax.broadcasted_iota(jnp.int32, (tm, tm), 0)).astype(jnp.bfloat16)
    idx_row = pl.BlockSpec((1, 1, tm), lambda i: (prev(i), 0, 0))
    return pl.pallas_call(
        _merge_kernel,
        grid=(nt + 1,),
        in_specs=[row(D_MODEL), row(HW), row(HW), row(HW), memspec, memspec, row(3 * D_MODEL),
                  const((HW, D_MODEL)), const((HW, D_MODEL)), const((HW, D_MODEL)), const((D_MODEL, D_MODEL)),
                  const((1, D_MODEL)), const((D_MODEL, LANES)), const((D_MODEL, LANES)), const((1, LANES)),
                  const((tm, tm))],
        out_specs=[row(D_MODEL), row(D_MODEL), pl.BlockSpec((tm, LANES), lambda i: (prev(i), 0)), idx_row, idx_row,
                   const((1, N_CLASSES))],
        out_shape=[jax.ShapeDtypeStruct((n_tok, D_MODEL), jnp.float32),
                   jax.ShapeDtypeStruct((n_tok, D_MODEL), jnp.float32),
                   jax.ShapeDtypeStruct((n_tok, LANES), jnp.float32),
                   jax.ShapeDtypeStruct((nt, 1, tm), jnp.int32),
                   jax.ShapeDtypeStruct((nt, 1, tm), jnp.int32),
                   jax.ShapeDtypeStruct((1, N_CLASSES), jnp.int32)],
        scratch_shapes=[pltpu.VMEM((1, N_CLASSES), jnp.float32), pltpu.VMEM((2, tm, LANES), jnp.float32)],
        compiler_params=_cparams(("arbitrary",)),
        name="merge",
    )(x2, ya, yb, qc, kmem, vmem, gate, wd, wf, wm, wo, fg, rhi, rlo, rb, tri)


_EXPERT_LANE0 = N_GROUPS


def _route_tile(lg, counted, tri_ref, w_o, cls_o, rank_o, cnt_o, carry_ref):
    lanei = lax.broadcasted_iota(jnp.int32, lg.shape, 1)
    lane = lanei.astype(jnp.float32)
    ninf = jnp.float32(-jnp.inf)
    big = jnp.float32(LANES)

    def first_max(vals):
        top = jnp.max(vals, axis=1, keepdims=True)
        return top, jnp.min(jnp.where(vals == top, lane, big), axis=1, keepdims=True)

    is_group = lanei < N_GROUPS
    gmax, gidx = first_max(jnp.where(is_group, lg, ninf))
    gate = 1.0 / jnp.sum(jnp.where(is_group, jnp.exp(lg - gmax), 0.0), axis=1, keepdims=True)
    lo = _EXPERT_LANE0 + EXPERTS_PER_GROUP * gidx
    el = jnp.where((lane >= lo) & (lane < lo + EXPERTS_PER_GROUP), lg, ninf)
    t1, i1 = first_max(el)
    t2, i2 = first_max(jnp.where(lane == i1, ninf, el))
    e2 = jnp.exp(t2 - t1)
    w1 = gate * (1.0 / (1.0 + e2))
    w2 = gate * (e2 / (1.0 + e2))
    first_low = i1 < i2
    ea = jnp.minimum(i1, i2) - lo
    eb = jnp.maximum(i1, i2) - lo
    wa = jnp.where(first_low, w1, w2)
    wb = jnp.where(first_low, w2, w1)
    cls = (EXPERTS_PER_GROUP * EXPERTS_PER_GROUP) * gidx + EXPERTS_PER_GROUP * ea + eb

    cls_lane = lax.broadcasted_iota(jnp.int32, (lg.shape[0], N_CLASSES), 1).astype(jnp.float32)
    onehot = cls_lane == cls
    before = _dot(tri_ref[...], jnp.where(onehot, 1.0, 0.0).astype(jnp.bfloat16)) + carry_ref[...]
    rank = jnp.sum(jnp.where(onehot, before, 0.0), axis=1, keepdims=True)
    carry_ref[...] = carry_ref[...] + jnp.where(counted, jnp.sum(jnp.where(onehot, 1.0, 0.0), axis=0, keepdims=True), 0.0)
    cnt_o[...] = carry_ref[...].astype(jnp.int32)

    w_o[...] = jnp.where(lanei == 0, wa, jnp.where(lanei == 1, wb, 0.0))
    idx_t = jnp.where(lanei == 0, cls, jnp.where(lanei == 1, rank, 0.0)).T
    cls_o[0] = idx_t[0:1, :].astype(jnp.int32)
    rank_o[0] = idx_t[1:2, :].astype(jnp.int32)


ROW_DMA_UNROLL = 8


def _dest_row(pstart_ref, cls_ref, rank_ref, r):
    return pstart_ref[cls_ref[0, 0, r]] + rank_ref[0, 0, r]


def _for_rows(tm, fn):
    def trip(g, c):
        for u in range(ROW_DMA_UNROLL):
            fn(g * ROW_DMA_UNROLL + u, u)
        return c

    lax.fori_loop(0, tm // ROW_DMA_UNROLL, trip, 0)


def _dispatch_kernel(tm, pstart_ref, pend_ref, cls_ref, rank_ref, pcls_ref, prank_ref, h2_ref, w_ref, rows_out,
                     tile_ref, zero_ref, sems):
    i = pl.program_id(0)
    slot = i % 2

    def row_copy(c_ref, r_ref, s, r):
        dest = _dest_row(pstart_ref, c_ref, r_ref, r)
        return pltpu.make_async_copy(tile_ref.at[s, pl.ds(r, 1)], rows_out.at[pl.ds(dest, 1)], sems.at[s])

    @pl.when(i == 0)
    def _():
        zero_ref[...] = jnp.zeros_like(zero_ref)

        def zero_copy(c):
            start = pl.multiple_of(pend_ref[c] - EXPERT_BLOCK, EXPERT_BLOCK)
            return pltpu.make_async_copy(zero_ref, rows_out.at[pl.ds(start, EXPERT_BLOCK)], sems.at[2])

        def for_spans(fn):
            def trip(c, carry):
                @pl.when(pend_ref[c] > pstart_ref[c])
                def _():
                    fn(c)
                return carry

            lax.fori_loop(0, N_CLASSES, trip, 0)

        def tail_copy(b):
            start = pl.multiple_of(b * EXPERT_BLOCK, EXPERT_BLOCK)
            return pltpu.make_async_copy(zero_ref, rows_out.at[pl.ds(start, EXPERT_BLOCK)], sems.at[2])

        def for_tail(fn):
            def trip(b, carry):
                fn(b)
                return carry

            lax.fori_loop(pend_ref[N_CLASSES - 1] // EXPERT_BLOCK, rows_out.shape[0] // EXPERT_BLOCK, trip, 0)

        for_spans(lambda c: zero_copy(c).start())
        for_tail(lambda b: tail_copy(b).start())
        for_spans(lambda c: zero_copy(c).wait())
        for_tail(lambda b: tail_copy(b).wait())

    tile_ref[slot, :, 0:D_MODEL] = h2_ref[...]
    tile_ref[slot, :, D_MODEL:ROW_W] = w_ref[...]
    _for_rows(tm, lambda r, u: row_copy(cls_ref, rank_ref, slot, r).start(priority=u % 2))

    @pl.when(i > 0)
    def _():
        _for_rows(tm, lambda r, u: row_copy(pcls_ref, prank_ref, 1 - slot, r).wait())

    @pl.when(i == pl.num_programs(0) - 1)
    def _():
        _for_rows(tm, lambda r, u: row_copy(cls_ref, rank_ref, slot, r).wait())


def _dispatch(pstarts, pends, cls3, rank3, h2, wslab, n_rows, tm):
    n_tok = h2.shape[0]
    idx = lambda f: pl.BlockSpec((1, 1, tm), f, memory_space=pltpu.SMEM)
    grid_spec = pltpu.PrefetchScalarGridSpec(
        num_scalar_prefetch=2,
        grid=(n_tok // tm,),
        in_specs=[idx(lambda i, ps, pe: (i, 0, 0)), idx(lambda i, ps, pe: (i, 0, 0)),
                  idx(lambda i, ps, pe: (jnp.maximum(i - 1, 0), 0, 0)),
                  idx(lambda i, ps, pe: (jnp.maximum(i - 1, 0), 0, 0)),
                  pl.BlockSpec((tm, D_MODEL), lambda i, ps, pe: (i, 0)),
                  pl.BlockSpec((tm, LANES), lambda i, ps, pe: (i, 0))],
        out_specs=pl.BlockSpec(memory_space=pl.ANY),
        scratch_shapes=[pltpu.VMEM((2, tm, ROW_W), jnp.float32), pltpu.VMEM((EXPERT_BLOCK, ROW_W), jnp.float32),
                        pltpu.SemaphoreType.DMA((3,))],
    )
    return pl.pallas_call(
        functools.partial(_dispatch_kernel, tm),
        grid_spec=grid_spec,
        out_shape=jax.ShapeDtypeStruct((n_rows, ROW_W), jnp.float32),
        compiler_params=_cparams(("arbitrary",)),
        name="dispatch",
    )(pstarts, pends, cls3, rank3, cls3, rank3, h2, wslab)


def _combine_kernel(tm, pstart_ref, cls_ref, rank_ref, ncls_ref, nrank_ref, x1_ref, y_ref, o_ref, buf_ref, sems):
    i = pl.program_id(0)
    slot = i % 2

    def row_copy(c_ref, r_ref, s, r):
        src = _dest_row(pstart_ref, c_ref, r_ref, r)
        return pltpu.make_async_copy(y_ref.at[pl.ds(src, 1)], buf_ref.at[s, pl.ds(r, 1)], sems.at[s])

    @pl.when(i == 0)
    def _():
        _for_rows(tm, lambda r, u: row_copy(cls_ref, rank_ref, slot, r).start(priority=u % 2))

    @pl.when(i < pl.num_programs(0) - 1)
    def _():
        _for_rows(tm, lambda r, u: row_copy(ncls_ref, nrank_ref, 1 - slot, r).start(priority=u % 2))

    _for_rows(tm, lambda r, u: row_copy(cls_ref, rank_ref, slot, r).wait())
    o_ref[...] = x1_ref[...] + buf_ref[slot]


def _combine(pstarts, cls3, rank3, x1, y, tm):
    n_tok = x1.shape[0]
    nt = n_tok // tm
    idx = lambda f: pl.BlockSpec((1, 1, tm), f, memory_space=pltpu.SMEM)
    grid_spec = pltpu.PrefetchScalarGridSpec(
        num_scalar_prefetch=1,
        grid=(nt,),
        in_specs=[idx(lambda i, ps: (i, 0, 0)), idx(lambda i, ps: (i, 0, 0)),
                  idx(lambda i, ps: (jnp.minimum(i + 1, nt - 1), 0, 0)),
                  idx(lambda i, ps: (jnp.minimum(i + 1, nt - 1), 0, 0)),
                  pl.BlockSpec((tm, D_MODEL), lambda i, ps: (i, 0)),
                  pl.BlockSpec(memory_space=pl.ANY)],
        out_specs=pl.BlockSpec((tm, D_MODEL), lambda i, ps: (i, 0)),
        scratch_shapes=[pltpu.VMEM((2, tm, D_MODEL), jnp.float32), pltpu.SemaphoreType.DMA((2,))],
    )
    return pl.pallas_call(
        functools.partial(_combine_kernel, tm),
        grid_spec=grid_spec,
        out_shape=jax.ShapeDtypeStruct(x1.shape, jnp.float32),
        compiler_params=_cparams(("arbitrary",)),
        name="combine",
    )(pstarts, cls3, rank3, cls3, rank3, x1, y)


def _expert_kernel(ea_ref, eb_ref, nused_ref, rows_ref, ua_ref, ub_ref, da_ref, db_ref, y_ref):
    b = pl.program_id(0)

    @pl.when(b < nused_ref[0])
    def _():
        h = rows_ref[:, 0:D_MODEL].astype(jnp.bfloat16)

        def expert(up_ref, down_ref, w):
            u = _dot(h, up_ref[0])
            act = (jax.nn.silu(u[:, 0:EXPERT_FF]) * u[:, EXPERT_FF:]).astype(jnp.bfloat16)
            return w * _dot(act, down_ref[0])

        y_ref[...] = (expert(ua_ref, da_ref, rows_ref[:, D_MODEL:D_MODEL + 1])
                      + expert(ub_ref, db_ref, rows_ref[:, D_MODEL + 1:D_MODEL + 2]))

    @pl.when(b >= nused_ref[0])
    def _():
        y_ref[...] = jnp.zeros_like(y_ref)


def _experts(blk_ea, blk_eb, nused, rows, w_up, w_down):
    n_blocks = rows.shape[0] // EXPERT_BLOCK
    grid_spec = pltpu.PrefetchScalarGridSpec(
        num_scalar_prefetch=3,
        grid=(n_blocks,),
        in_specs=[
            pl.BlockSpec((EXPERT_BLOCK, ROW_W), lambda b, ea, eb, nu: (jnp.minimum(b, nu[0] - 1), 0)),
            pl.BlockSpec((1, D_MODEL, 2 * EXPERT_FF), lambda b, ea, eb, nu: (ea[b], 0, 0)),
            pl.BlockSpec((1, D_MODEL, 2 * EXPERT_FF), lambda b, ea, eb, nu: (eb[b], 0, 0)),
            pl.BlockSpec((1, EXPERT_FF, D_MODEL), lambda b, ea, eb, nu: (ea[b], 0, 0)),
            pl.BlockSpec((1, EXPERT_FF, D_MODEL), lambda b, ea, eb, nu: (eb[b], 0, 0)),
        ],
        out_specs=pl.BlockSpec((EXPERT_BLOCK, D_MODEL), lambda b, ea, eb, nu: (b, 0)),
    )
    return pl.pallas_call(
        _expert_kernel,
        grid_spec=grid_spec,
        out_shape=jax.ShapeDtypeStruct((rows.shape[0], D_MODEL), jnp.float32),
        compiler_params=_cparams(("arbitrary",)),
        name="experts",
    )(blk_ea, blk_eb, nused, rows, w_up, w_up, w_down, w_down)


def _pick_tile(n, pref):
    t = min(n, pref)
    while n % t:
        t //= 2
    return t


def kernel(x, mem, positions, attn_norm_g, w_in, diff_qnorm_g, diff_knorm_g, diff_lambda, diff_subln_g,
           fox_qnorm_g, fox_knorm_g, fox_forget_b, mem_norm_g, w_mem_kv, mem_qnorm_g, mem_knorm_g, w_o_diff,
           w_o_fox, w_o_mem, w_out, ffn_norm_g, w_router_group, b_router_group, w_router_expert,
           b_router_expert, w_up, w_down):
    n_batch, seq, _ = x.shape
    mem_len = mem.shape[1]
    n_tok = n_batch * seq
    bf = jnp.bfloat16
    tm = _pick_tile(seq, 512)
    tq = _pick_tile(seq, 512)
    tk = _pick_tile(tq, 512)
    tiles_per_seq = seq // tm
    x2 = x.reshape(n_tok, D_MODEL)

    wi = w_in[0]
    o = 0
    parts = {}
    for name, width in (("qa", HW), ("ka", HW), ("va", HW), ("qb", HW), ("kb", HW), ("vb", HW),
                        ("fb", FOX_HEADS), ("qc", HW), ("g", 3 * D_MODEL)):
        parts[name] = wi[:, o:o + width]
        o += width
    w_main = jnp.concatenate([parts[k] for k in ("qa", "ka", "va", "qb", "kb", "vb", "qc", "g")], axis=1).astype(bf)
    w_f = jnp.pad(parts["fb"], ((0, 0), (0, LANES - FOX_HEADS))).astype(bf)
    fb = jnp.pad(fox_forget_b[0], (0, LANES - FOX_HEADS)).reshape(1, LANES)
    tile_gain = lambda g, n: jnp.tile(g, n).reshape(1, -1)
    gains = (tile_gain(diff_qnorm_g[0], _NORM_CHUNK // HEAD_DIM), tile_gain(diff_knorm_g[0], _NORM_CHUNK // HEAD_DIM),
             tile_gain(fox_qnorm_g[0], _NORM_CHUNK // HEAD_DIM), tile_gain(fox_knorm_g[0], _NORM_CHUNK // HEAD_DIM),
             mem_qnorm_g[0].reshape(1, MEM_DIM))

    kmem, vmem = _mem_kv(mem.reshape(n_batch * mem_len, D_MODEL), mem_norm_g[0].reshape(1, D_MODEL),
                         w_mem_kv[0].astype(bf), mem_knorm_g[0].reshape(1, MEM_DIM), n_batch, mem_len)

    qa, ka, va, qb, kb, vb, qc, gate, cum = _in_proj(
        x2, attn_norm_g[0].reshape(1, D_MODEL), w_main, w_f, fb, positions.reshape(n_tok, 1).astype(jnp.int32),
        gains, tm, tiles_per_seq)

    cum_t = cum[:, :FOX_HEADS].reshape(n_batch, seq, FOX_HEADS // 2, 2).transpose(0, 2, 3, 1)
    yb = _fox_attn(qb, kb, vb, cum_t, n_batch, seq, tq, tk)
    ya = _diff_attn(qa, ka, va, diff_lambda[0], diff_subln_g[0].reshape(1, LANES), n_batch, seq, tq, tk)

    w_r = jnp.concatenate([w_router_group[0], w_router_expert[0]], axis=1)
    w_r = jnp.pad(w_r, ((0, 0), (0, LANES - w_r.shape[1])))
    r_hi = w_r.astype(bf)
    r_lo = (w_r - r_hi.astype(jnp.float32)).astype(bf)
    r_b = jnp.pad(jnp.concatenate([b_router_group[0], b_router_expert[0]]), (0, LANES - N_GROUPS - N_EXPERTS))
    x1, h2, wslab, cls3, rank3, counts = _merge(x2, ya, yb, qc, kmem, vmem, gate, w_o_diff[0].astype(bf), w_o_fox[0].astype(bf),
                            w_o_mem[0].astype(bf), w_out[0].astype(bf), ffn_norm_g[0].reshape(1, D_MODEL),
                            r_hi, r_lo, r_b.reshape(1, LANES), tm, tiles_per_seq, mem_len)

    blk = EXPERT_BLOCK
    n_blocks = -(-(n_tok + N_LIVE_CLASSES * (blk - 1)) // blk)
    counts = counts[0]
    padded = (counts + blk - 1) // blk * blk
    pends = jnp.cumsum(padded)
    pstarts = pends - padded
    blk_cls = jnp.minimum(jnp.searchsorted(pends, jnp.arange(n_blocks, dtype=jnp.int32) * blk, side="right"),
                          N_CLASSES - 1).astype(jnp.int32)
    per_group = EXPERTS_PER_GROUP * EXPERTS_PER_GROUP
    blk_ea = blk_cls // EXPERTS_PER_GROUP
    blk_eb = (blk_cls // per_group) * EXPERTS_PER_GROUP + blk_cls % EXPERTS_PER_GROUP
    nused = (pends[-1] // blk).astype(jnp.int32).reshape(1)
    pstarts = pstarts.astype(jnp.int32)

    rows = _dispatch(pstarts, pends.astype(jnp.int32), cls3, rank3, h2, wslab, n_blocks * blk, tm)
    y = _experts(blk_ea, blk_eb, nused, rows, w_up[0].astype(bf), w_down[0].astype(bf))
    out = _combine(pstarts, cls3, rank3, x1, y, tm)
    return out.reshape(x.shape)
```

```python
import functools
import math

import jax
import jax.numpy as jnp
from jax import lax
from jax.experimental import pallas as pl
from jax.experimental.pallas import tpu as pltpu

D_MODEL = 1024
HEAD_DIM = 64
DIFF_HEADS = 4
FOX_HEADS = 8
MEM_HEADS = 4
MEM_DIM = 128
ROPE_THETA = 500000.0
ROPE_DIM = HEAD_DIM // 4
N_GROUPS = 4
EXPERTS_PER_GROUP = 8
N_EXPERTS = N_GROUPS * EXPERTS_PER_GROUP
EXPERT_FF = 512
EPS = 1e-6
NEG_INF = -1e30
LAMBDA_INIT = 0.8 - 0.6 * math.exp(-0.3 * 0)
LOG2E = 1.4426950408889634

LANES = 128
VMEM_LIMIT_BYTES = 56 * 1024 * 1024

HW = 512
N_CLASSES = 256
N_LIVE_CLASSES = N_GROUPS * (EXPERTS_PER_GROUP * (EXPERTS_PER_GROUP - 1) // 2)
ROW_W = D_MODEL + LANES
EXPERT_BLOCK = 256


def _cparams(sem):
    return pltpu.CompilerParams(dimension_semantics=sem, vmem_limit_bytes=VMEM_LIMIT_BYTES)


def _rms_rows(x, g):
    return x * lax.rsqrt(jnp.mean(x * x, axis=-1, keepdims=True) + EPS) * g


def _dot(a, b):
    return jnp.dot(a, b, preferred_element_type=jnp.float32)


def _dot_nt(a, b):
    return lax.dot_general(a, b, (((1,), (1,)), ((), ())), preferred_element_type=jnp.float32)


def _mem_kv_kernel(mem_ref, g_ref, w_ref, kg_ref, k_out, v_out):
    mn = _rms_rows(mem_ref[...], g_ref[...]).astype(jnp.bfloat16)
    kv = _dot(mn, w_ref[...])
    for h in range(MEM_HEADS):
        kh = kv[:, h * MEM_DIM:(h + 1) * MEM_DIM]
        k_out[:, h * MEM_DIM:(h + 1) * MEM_DIM] = _rms_rows(kh, kg_ref[...]).astype(jnp.bfloat16)
    v_out[...] = kv[:, HW:].astype(jnp.bfloat16)


def _mem_kv(mem2, g, w_kv, kg, n_batch, mem_len):
    return pl.pallas_call(
        _mem_kv_kernel,
        grid=(n_batch,),
        in_specs=[
            pl.BlockSpec((mem_len, D_MODEL), lambda b: (b, 0)),
            pl.BlockSpec((1, D_MODEL), lambda b: (0, 0)),
            pl.BlockSpec((D_MODEL, 2 * HW), lambda b: (0, 0)),
            pl.BlockSpec((1, MEM_DIM), lambda b: (0, 0)),
        ],
        out_specs=[
            pl.BlockSpec((mem_len, HW), lambda b: (b, 0)),
            pl.BlockSpec((mem_len, HW), lambda b: (b, 0)),
        ],
        out_shape=[jax.ShapeDtypeStruct((n_batch * mem_len, HW), jnp.bfloat16)] * 2,
        compiler_params=_cparams(("arbitrary",)),
        name="mem_kv",
    )(mem2, g, w_kv, kg)


_C_QA, _C_KA, _C_VA, _C_QB, _C_KB, _C_VB, _C_QC, _C_G = (i * HW for i in range(8))
_MAIN_COLS = 7 * HW + 3 * D_MODEL
_NORM_CHUNK = 256
_HEADS_PER_CHUNK = _NORM_CHUNK // HEAD_DIM
FOX_SLAB = FOX_HEADS * LANES
BIAS_PARTS = 3


def _group_rms(r, bd, g):
    ss = _dot((r * r).astype(jnp.bfloat16), bd)
    return r * lax.rsqrt(ss * (1.0 / HEAD_DIM) + EPS) * g


def _in_proj_kernel(tiles_per_seq, x_ref, g_ref, w_ref, wf_ref, fb_ref, pos_ref, rinv_ref, ra_ref, rb_ref, bd_ref, tri_ref,
                    place_ref, gqa_ref, gka_ref, gqb_ref, gkb_ref, gqc_ref,
                    qa_o, ka_o, va_o, qb_o, kb_o, vb_o, qc_o, gate_o, cum_o, carry_ref):
    i = pl.program_id(0)

    @pl.when(i == 0)
    def _():
        carry_ref[...] = jnp.zeros_like(carry_ref)

    h = _rms_rows(x_ref[...], g_ref[...]).astype(jnp.bfloat16)
    bd = bd_ref[...]
    nchunk = HW // _NORM_CHUNK
    lane = lax.broadcasted_iota(jnp.int32, (h.shape[0], LANES), 1)

    def split3(v):
        p1 = v.astype(jnp.bfloat16)
        r1 = v - p1.astype(jnp.float32)
        p2 = r1.astype(jnp.bfloat16)
        return p1, p2, (r1 - p2.astype(jnp.float32)).astype(jnp.bfloat16)

    ang = pos_ref[...].astype(jnp.float32) * rinv_ref[...]
    sin = jnp.sin(ang)
    rep = _NORM_CHUNK // LANES
    rope_c = jnp.concatenate([jnp.cos(ang)] * rep, axis=1)
    rope_a = jnp.concatenate([sin * ra_ref[...]] * rep, axis=1)
    rope_b = jnp.concatenate([sin * rb_ref[...]] * rep, axis=1)
    half = ROPE_DIM // 2

    def rope(y):
        return (y * rope_c + pltpu.roll(y, _NORM_CHUNK - half, axis=1) * rope_a
                + pltpu.roll(y, half, axis=1) * rope_b)

    def normed_chunk(r, c, gain, use_rope, scale):
        y = _group_rms(r[:, c * _NORM_CHUNK:(c + 1) * _NORM_CHUNK], bd, gain)
        if use_rope:
            y = rope(y)
        return y if scale is None else y * scale

    def packed_branch(col0, out, gain_ref, scale):
        gain = gain_ref[...]
        r = _dot(h, w_ref[:, col0:col0 + HW])
        for c in range(nchunk):
            out[:, c * _NORM_CHUNK:(c + 1) * _NORM_CHUNK] = normed_chunk(r, c, gain, True, scale).astype(jnp.bfloat16)

    def slab_branch(col0, out, gain_ref, scale, spare):
        gain = gain_ref[...]
        r = _dot(h, w_ref[:, col0:col0 + HW])
        for c in range(nchunk):
            y = normed_chunk(r, c, gain, False, scale)
            for hh in range(_HEADS_PER_CHUNK):
                src = y[:, (hh // 2) * LANES:(hh // 2 + 1) * LANES]
                if hh % 2:
                    src = pltpu.roll(src, HEAD_DIM, axis=1)
                head = c * _HEADS_PER_CHUNK + hh
                out[:, head * LANES:(head + 1) * LANES] = jnp.where(lane < HEAD_DIM, src, spare(head)).astype(jnp.bfloat16)

    q_scale = HEAD_DIM ** -0.5 * LOG2E
    slab_branch(_C_QB, qb_o, gqb_ref, q_scale, lambda head: jnp.where(
        (lane >= HEAD_DIM) & (lane < HEAD_DIM + BIAS_PARTS), 1.0, 0.0))

    z = _dot(h, wf_ref[...]) + fb_ref[...]
    logf = -(jnp.maximum(-z, 0.0) + jnp.log1p(jnp.exp(-jnp.abs(z))))
    va_o[...] = _dot(h, w_ref[:, _C_VA:_C_VA + HW]).astype(jnp.bfloat16)
    vb_o[...] = _dot(h, w_ref[:, _C_VB:_C_VB + HW]).astype(jnp.bfloat16)

    tri = tri_ref[...]
    carry = jnp.where(i % tiles_per_seq == 0, 0.0, carry_ref[...])
    c3 = _dot(tri, jnp.concatenate(split3(logf), axis=1))
    cum = (c3[:, 0:LANES] + c3[:, LANES:2 * LANES]) + c3[:, 2 * LANES:3 * LANES] + carry
    cum_o[...] = cum
    carry_ref[...] = cum[cum.shape[0] - 1:, :]

    gqc = gqc_ref[...]
    rc = _dot(h, w_ref[:, _C_QC:_C_QC + HW])
    for hd in range(MEM_HEADS):
        sl = slice(hd * MEM_DIM, (hd + 1) * MEM_DIM)
        qc_o[:, sl] = (_rms_rows(rc[:, sl], gqc) * MEM_DIM ** -0.5).astype(jnp.bfloat16)

    packed_branch(_C_QA, qa_o, gqa_ref, q_scale)

    pieces = [p.astype(jnp.float32) for p in split3(jnp.where(lane < FOX_HEADS, cum * (-LOG2E), 0.0))]
    packed = pieces[0]
    for k in range(1, BIAS_PARTS):
        packed = packed + pltpu.roll(pieces[k], k * FOX_HEADS, axis=1)
    kbias = _dot(packed.astype(jnp.bfloat16), place_ref[...])
    packed_branch(_C_KA, ka_o, gka_ref, None)
    slab_branch(_C_KB, kb_o, gkb_ref, None, lambda head: kbias[:, head * LANES:(head + 1) * LANES])

    for c in range(3 * D_MODEL // HW):
        lo = _C_G + c * HW
        r = _dot(h, w_ref[:, lo:lo + HW])
        gate_o[:, c * HW:(c + 1) * HW] = jax.nn.sigmoid(r).astype(jnp.bfloat16)


def _in_proj(x2, g, w_main, w_f, fb, pos, gains, tm, tiles_per_seq):
    n_tok = x2.shape[0]
    half = ROPE_DIM // 2
    head_lane = jnp.arange(LANES, dtype=jnp.int32) % HEAD_DIM
    inv = ROPE_THETA ** (-jnp.arange(0, ROPE_DIM, 2, dtype=jnp.float32) / ROPE_DIM)
    rope_inv = jnp.where(head_lane < ROPE_DIM, inv[head_lane % half], 0.0).reshape(1, LANES)
    rope_sa = jnp.where(head_lane < half, -1.0, 0.0).reshape(1, LANES)
    rope_sb = jnp.where((head_lane >= half) & (head_lane < ROPE_DIM), 1.0, 0.0).reshape(1, LANES)
    bd = (lax.broadcasted_iota(jnp.int32, (_NORM_CHUNK, _NORM_CHUNK), 0) // HEAD_DIM
          == lax.broadcasted_iota(jnp.int32, (_NORM_CHUNK, _NORM_CHUNK), 1) // HEAD_DIM).astype(jnp.bfloat16)
    tri = (lax.broadcasted_iota(jnp.int32, (tm, tm), 1)
           <= lax.broadcasted_iota(jnp.int32, (tm, tm), 0)).astype(jnp.bfloat16)
    src = lax.broadcasted_iota(jnp.int32, (LANES, FOX_SLAB), 0)
    dst = lax.broadcasted_iota(jnp.int32, (LANES, FOX_SLAB), 1)
    place = ((src < BIAS_PARTS * FOX_HEADS)
             & (dst == (src % FOX_HEADS) * LANES + HEAD_DIM + src // FOX_HEADS)).astype(jnp.bfloat16)
    const = lambda shape: pl.BlockSpec(shape, lambda i: (0, 0))
    resident = lambda shape: pl.BlockSpec(shape, lambda i: (0, 0), pipeline_mode=pl.Buffered(1))
    row = lambda w: pl.BlockSpec((tm, w), lambda i: (i, 0))
    slab = lambda w: jax.ShapeDtypeStruct((n_tok, w), jnp.bfloat16)
    return pl.pallas_call(
        functools.partial(_in_proj_kernel, tiles_per_seq),
        grid=(n_tok // tm,),
        in_specs=[row(D_MODEL), const((1, D_MODEL)), resident((D_MODEL, _MAIN_COLS)), const((D_MODEL, LANES)),
                  const((1, LANES)), row(1), const((1, LANES)), const((1, LANES)), const((1, LANES)),
                  const((_NORM_CHUNK, _NORM_CHUNK)), const((tm, tm)), const((LANES, FOX_SLAB)),
                  const((1, _NORM_CHUNK)), const((1, _NORM_CHUNK)), const((1, _NORM_CHUNK)),
                  const((1, _NORM_CHUNK)), const((1, MEM_DIM))],
        out_specs=[row(HW), row(HW), row(HW), row(FOX_SLAB), row(FOX_SLAB), row(HW), row(HW),
                   row(3 * D_MODEL), row(LANES)],
        out_shape=[slab(HW), slab(HW), slab(HW), slab(FOX_SLAB), slab(FOX_SLAB), slab(HW), slab(HW),
                   slab(3 * D_MODEL), jax.ShapeDtypeStruct((n_tok, LANES), jnp.float32)],
        scratch_shapes=[pltpu.VMEM((1, LANES), jnp.float32)],
        compiler_params=_cparams(("arbitrary",)),
        name="in_proj",
    )(x2, g, w_main, w_f, fb, pos, rope_inv, rope_sa, rope_sb, bd, tri, place, *gains)


FOX_V_ROWS = 80
DIFF_V_ROWS = 144


Q_SPLIT = 2


def _flash_transposed(qs, k_of, v_of, shifts, tq, v_rows, finish):
    qi = pl.program_id(2)
    tqp = tq // Q_SPLIT
    parts = [(w, h) for w in range(len(qs)) for h in range(Q_SPLIT)]
    q_parts = [qs[w][h * tqp:(h + 1) * tqp] for (w, h) in parts]
    shift_parts = [0.0 if shifts[w] is None else shifts[w][:, h * tqp:(h + 1) * tqp] for (w, h) in parts]

    def block(k0, n, diag, state):
        def n_keys(h):
            return n if diag is None else min(n, diag + (h + 1) * tqp)

        sc = []
        for c, (w, h) in enumerate(parts):
            s = _dot_nt(k_of(w, k0, n_keys(h)), q_parts[c])
            if diag is not None:
                key = lax.broadcasted_iota(jnp.int32, s.shape, 0) - diag
                qry = lax.broadcasted_iota(jnp.int32, s.shape, 1) + h * tqp
                s = jnp.where(key <= qry, s, NEG_INF)
            sc.append((s, jnp.max(s, axis=0, keepdims=True)))
        out = []
        for c, (w, h) in enumerate(parts):
            (s, s_max), (m_old, acc_old), shift = sc[c], state[c], shift_parts[c]
            m_new = jnp.maximum(m_old, s_max + shift)
            p = jnp.exp2(s - (m_new - shift))
            acc_new = jnp.exp2(m_old - m_new) * acc_old + _dot(v_of(w, k0, n_keys(h)), p.astype(jnp.bfloat16))
            out.append((m_new, acc_new))
        return tuple(out)

    def done(state):
        finish([jnp.concatenate([state[w * Q_SPLIT + h][1] for h in range(Q_SPLIT)], axis=1)
                for w in range(len(qs))])

    init = tuple((jnp.full((1, tqp), NEG_INF, jnp.float32), jnp.zeros((v_rows, tqp), jnp.float32)) for _ in parts)
    state = lax.fori_loop(0, qi // 2, lambda j, st: block(pl.multiple_of(j * 2 * tq, 2 * tq), 2 * tq, None, st), init)

    @pl.when(qi % 2 == 1)
    def _():
        done(block(pl.multiple_of((qi - 1) * tq, tq), 2 * tq, tq, state))

    @pl.when(qi % 2 == 0)
    def _():
        done(block(pl.multiple_of(qi * tq, tq), tq, 0, state))


ATTN_CHAINS = 4


def _fill_values_t(v_ref, vt_ref, head_w, tk):
    n_heads, v_rows, seq = vt_ref.shape
    tail = lax.broadcasted_iota(jnp.int32, (v_rows - head_w, seq), 0)
    ones_row = jnp.where(tail == 0, 1.0, 0.0).astype(jnp.bfloat16)
    for h in range(n_heads):
        vt_ref[h, head_w:v_rows, :] = ones_row
    for c in range(seq // tk):
        blk_t = v_ref[c * tk:(c + 1) * tk, :].astype(jnp.float32).T
        for h in range(n_heads):
            vt_ref[h, 0:head_w, c * tk:(c + 1) * tk] = blk_t[h * head_w:(h + 1) * head_w].astype(jnp.bfloat16)

FOX_STEP_HEADS = ATTN_CHAINS
DIFF_STEP_HEADS = ATTN_CHAINS // 2


def _fox_kernel(tq, tk, q_ref, k_ref, v_ref, cq_ref, o_ref, vt_ref):
    heads = range(FOX_STEP_HEADS)

    @pl.when(pl.program_id(2) == 0)
    def _():
        _fill_values_t(v_ref, vt_ref, HEAD_DIM, tk)

    qs = [q_ref[:, w * LANES:(w + 1) * LANES] for w in heads]
    shifts = [cq_ref[0, w // 2, w % 2:w % 2 + 1, :] * LOG2E for w in heads]

    def finish(accs):
        o_t = jnp.concatenate([a[0:HEAD_DIM] / a[HEAD_DIM:HEAD_DIM + 1] for a in accs], axis=0)
        o_ref[...] = o_t.T.astype(jnp.bfloat16)

    _flash_transposed(qs, lambda w, k0, n: k_ref[pl.ds(k0, n), w * LANES:(w + 1) * LANES],
                      lambda w, k0, n: vt_ref[w, :, pl.ds(k0, n)], shifts, tq, FOX_V_ROWS, finish)


def _fox_attn(qb, kb, vb, cum_t, n_batch, seq, tq, tk):
    nq = seq // tq
    hs = FOX_STEP_HEADS
    return pl.pallas_call(
        functools.partial(_fox_kernel, tq, tk),
        grid=(n_batch, FOX_HEADS // hs, nq),
        in_specs=[
            pl.BlockSpec((tq, hs * LANES), lambda b, j, i: (b * nq + i, j)),
            pl.BlockSpec((seq, hs * LANES), lambda b, j, i: (b, j)),
            pl.BlockSpec((seq, hs * HEAD_DIM), lambda b, j, i: (b, j)),
            pl.BlockSpec((1, hs // 2, 2, tq), lambda b, j, i: (b, j, 0, i)),
        ],
        out_specs=pl.BlockSpec((tq, hs * HEAD_DIM), lambda b, j, i: (b * nq + i, j)),
        out_shape=jax.ShapeDtypeStruct((n_batch * seq, HW), jnp.bfloat16),
        scratch_shapes=[pltpu.VMEM((hs, FOX_V_ROWS, seq), jnp.bfloat16)],
        compiler_params=_cparams(("arbitrary", "arbitrary", "arbitrary")),
        name="fox_attn",
    )(qb, kb, vb, cum_t)


def _diff_kernel(tq, tk, q_ref, k_ref, v_ref, lam_ref, g_ref, o_ref, vt_ref):
    @pl.when(pl.program_id(2) == 0)
    def _():
        _fill_values_t(v_ref, vt_ref, 2 * HEAD_DIM, tk)

    lp = lam_ref[...]
    lam = (jnp.exp(jnp.sum(lp[0:1] * lp[1:2], axis=1, keepdims=True))
           - jnp.exp(jnp.sum(lp[2:3] * lp[3:4], axis=1, keepdims=True)) + LAMBDA_INIT)
    lane = lax.broadcasted_iota(jnp.int32, (tq, LANES), 1)
    qs = []
    for hd in range(DIFF_STEP_HEADS):
        q = q_ref[:, hd * LANES:(hd + 1) * LANES]
        zero = jnp.zeros_like(q)
        qs += [jnp.where(lane < HEAD_DIM, q, zero), jnp.where(lane >= HEAD_DIM, q, zero)]
    dv = 2 * HEAD_DIM

    def finish(accs):
        outs = []
        for hd in range(DIFF_STEP_HEADS):
            acc1, acc2 = accs[2 * hd], accs[2 * hd + 1]
            o_t = acc1[0:dv] / acc1[dv:dv + 1] - lam * (acc2[0:dv] / acc2[dv:dv + 1])
            outs.append(o_t * lax.rsqrt(jnp.mean(o_t * o_t, axis=0, keepdims=True) + EPS))
        gain = jnp.concatenate([g_ref[...]] * DIFF_STEP_HEADS, axis=1)
        o_ref[...] = (jnp.concatenate(outs, axis=0).T * gain * (1.0 - LAMBDA_INIT)).astype(jnp.bfloat16)

    _flash_transposed(qs, lambda w, k0, n: k_ref[pl.ds(k0, n), (w // 2) * LANES:(w // 2 + 1) * LANES],
                      lambda w, k0, n: vt_ref[w // 2, :, pl.ds(k0, n)], [None] * len(qs), tq, DIFF_V_ROWS, finish)


def _diff_attn(qa, ka, va, lam_params, subln_g, n_batch, seq, tq, tk):
    nq = seq // tq
    hs = DIFF_STEP_HEADS
    return pl.pallas_call(
        functools.partial(_diff_kernel, tq, tk),
        grid=(n_batch, DIFF_HEADS // hs, nq),
        in_specs=[
            pl.BlockSpec((tq, hs * LANES), lambda b, j, i: (b * nq + i, j)),
            pl.BlockSpec((seq, hs * LANES), lambda b, j, i: (b, j)),
            pl.BlockSpec((seq, hs * LANES), lambda b, j, i: (b, j)),
            pl.BlockSpec((4, HEAD_DIM), lambda b, j, i: (0, 0)),
            pl.BlockSpec((1, LANES), lambda b, j, i: (0, 0)),
        ],
        out_specs=pl.BlockSpec((tq, hs * LANES), lambda b, j, i: (b * nq + i, j)),
        out_shape=jax.ShapeDtypeStruct(qa.shape, jnp.bfloat16),
        scratch_shapes=[pltpu.VMEM((hs, DIFF_V_ROWS, seq), jnp.bfloat16)],
        compiler_params=_cparams(("arbitrary", "arbitrary", "arbitrary")),
        name="diff_attn",
    )(qa, ka, va, lam_params, subln_g)


def _merge_kernel(x_ref, ya_ref, yb_ref, qc_ref, km_ref, vm_ref, gate_ref, wd_ref, wf_ref, wm_ref, wo_ref,
                  fg_ref, rhi_ref, rlo_ref, rb_ref, tri_ref, x1_o, h2_o, w_o, cls_o, rank_o, cnt_o, carry_ref, lg_ref):
    i = pl.program_id(0)

    @pl.when(i == 0)
    def _():
        carry_ref[...] = jnp.zeros_like(carry_ref)
        lg_ref[...] = jnp.zeros_like(lg_ref)

    lg_prev = lg_ref[(i + 1) % 2]

    heads = []
    for h in range(MEM_HEADS):
        sl = slice(h * MEM_DIM, (h + 1) * MEM_DIM)
        s = _dot_nt(qc_ref[:, sl], km_ref[:, sl])
        p = jnp.exp(s - jnp.max(s, axis=1, keepdims=True))
        o = _dot(p.astype(jnp.bfloat16), vm_ref[:, sl]) / jnp.sum(p, axis=1, keepdims=True)
        heads.append(o.astype(jnp.bfloat16))
    yc = jnp.concatenate(heads, axis=1)

    merged = (gate_ref[:, 0:D_MODEL].astype(jnp.float32) * _dot(ya_ref[...], wd_ref[...])
              + gate_ref[:, D_MODEL:2 * D_MODEL].astype(jnp.float32) * _dot(yb_ref[...], wf_ref[...])
              + gate_ref[:, 2 * D_MODEL:3 * D_MODEL].astype(jnp.float32) * _dot(yc, wm_ref[...]))
    _route_tile(lg_prev, i > 0, tri_ref, w_o, cls_o, rank_o, cnt_o, carry_ref)
    x1 = x_ref[...] + _dot(merged.astype(jnp.bfloat16), wo_ref[...])
    x1_o[...] = x1
    h2 = _rms_rows(x1, fg_ref[...])
    h2_o[...] = h2
    a_hi = h2.astype(jnp.bfloat16)
    a_lo = (h2 - a_hi.astype(jnp.float32)).astype(jnp.bfloat16)
    lg_ref[i % 2] = (_dot(a_hi, rhi_ref[...]) + _dot(a_lo, rhi_ref[...]) + _dot(a_hi, rlo_ref[...])) + rb_ref[...]


def _merge(x2, ya, yb, qc, kmem, vmem, gate, wd, wf, wm, wo, fg, rhi, rlo, rb, tm, tiles_per_seq, mem_len):
    n_tok = x2.shape[0]
    nt = n_tok // tm
    cur = lambda i: jnp.minimum(i, nt - 1)
    prev = lambda i: jnp.maximum(i - 1, 0)
    const = lambda shape: pl.BlockSpec(shape, lambda i: (0, 0))
    row = lambda w: pl.BlockSpec((tm, w), lambda i: (cur(i), 0))
    memspec = pl.BlockSpec((mem_len, HW), lambda i: (cur(i) // tiles_per_seq, 0))
    tri = (lax.broadcasted_iota(jnp.int32, (tm, tm), 1)
           < lax.broadcasted_iota(jnp.int32, (tm, tm), 0)).astype(jnp.bfloat16)
    idx_row = pl.BlockSpec((1, 1, tm), lambda i: (prev(i), 0, 0))
    return pl.pallas_call(
        _merge_kernel,
        grid=(nt + 1,),
        in_specs=[row(D_MODEL), row(HW), row(HW), row(HW), memspec, memspec, row(3 * D_MODEL),
                  const((HW, D_MODEL)), const((HW, D_MODEL)), const((HW, D_MODEL)), const((D_MODEL, D_MODEL)),
                  const((1, D_MODEL)), const((D_MODEL, LANES)), const((D_MODEL, LANES)), const((1, LANES)),
                  const((tm, tm))],
        out_specs=[row(D_MODEL), row(D_MODEL), pl.BlockSpec((tm, LANES), lambda i: (prev(i), 0)), idx_row, idx_row,
                   const((1, N_CLASSES))],
        out_shape=[jax.ShapeDtypeStruct((n_tok, D_MODEL), jnp.float32),
                   jax.ShapeDtypeStruct((n_tok, D_MODEL), jnp.float32),
                   jax.ShapeDtypeStruct((n_tok, LANES), jnp.float32),
                   jax.ShapeDtypeStruct((nt, 1, tm), jnp.int32),
                   jax.ShapeDtypeStruct((nt, 1, tm), jnp.int32),
                   jax.ShapeDtypeStruct((1, N_CLASSES), jnp.int32)],
        scratch_shapes=[pltpu.VMEM((1, N_CLASSES), jnp.float32), pltpu.VMEM((2, tm, LANES), jnp.float32)],
        compiler_params=_cparams(("arbitrary",)),
        name="merge",
    )(x2, ya, yb, qc, kmem, vmem, gate, wd, wf, wm, wo, fg, rhi, rlo, rb, tri)


_EXPERT_LANE0 = N_GROUPS


def _route_tile(lg, counted, tri_ref, w_o, cls_o, rank_o, cnt_o, carry_ref):
    lanei = lax.broadcasted_iota(jnp.int32, lg.shape, 1)
    lane = lanei.astype(jnp.float32)
    ninf = jnp.float32(-jnp.inf)
    big = jnp.float32(LANES)

    def first_max(vals):
        top = jnp.max(vals, axis=1, keepdims=True)
        return top, jnp.min(jnp.where(vals == top, lane, big), axis=1, keepdims=True)

    is_group = lanei < N_GROUPS
    gmax, gidx = first_max(jnp.where(is_group, lg, ninf))
    gate = 1.0 / jnp.sum(jnp.where(is_group, jnp.exp(lg - gmax), 0.0), axis=1, keepdims=True)
    lo = _EXPERT_LANE0 + EXPERTS_PER_GROUP * gidx
    el = jnp.where((lane >= lo) & (lane < lo + EXPERTS_PER_GROUP), lg, ninf)
    t1, i1 = first_max(el)
    t2, i2 = first_max(jnp.where(lane == i1, ninf, el))
    e2 = jnp.exp(t2 - t1)
    w1 = gate * (1.0 / (1.0 + e2))
    w2 = gate * (e2 / (1.0 + e2))
    first_low = i1 < i2
    ea = jnp.minimum(i1, i2) - lo
    eb = jnp.maximum(i1, i2) - lo
    wa = jnp.where(first_low, w1, w2)
    wb = jnp.where(first_low, w2, w1)
    cls = (EXPERTS_PER_GROUP * EXPERTS_PER_GROUP) * gidx + EXPERTS_PER_GROUP * ea + eb

    cls_lane = lax.broadcasted_iota(jnp.int32, (lg.shape[0], N_CLASSES), 1).astype(jnp.float32)
    onehot = cls_lane == cls
    before = _dot(tri_ref[...], jnp.where(onehot, 1.0, 0.0).astype(jnp.bfloat16)) + carry_ref[...]
    rank = jnp.sum(jnp.where(onehot, before, 0.0), axis=1, keepdims=True)
    carry_ref[...] = carry_ref[...] + jnp.where(counted, jnp.sum(jnp.where(onehot, 1.0, 0.0), axis=0, keepdims=True), 0.0)
    cnt_o[...] = carry_ref[...].astype(jnp.int32)

    w_o[...] = jnp.where(lanei == 0, wa, jnp.where(lanei == 1, wb, 0.0))
    idx_t = jnp.where(lanei == 0, cls, jnp.where(lanei == 1, rank, 0.0)).T
    cls_o[0] = idx_t[0:1, :].astype(jnp.int32)
    rank_o[0] = idx_t[1:2, :].astype(jnp.int32)


ROW_DMA_UNROLL = 8


def _dest_kernel(cls_ref, rank_ref, ps_ref, dest_o):
    cls = cls_ref[0].astype(jnp.float32)
    sub = lax.broadcasted_iota(jnp.int32, (N_CLASSES, cls.shape[1]), 0).astype(jnp.float32)
    start = jnp.sum(jnp.where(sub == cls, ps_ref[...], 0.0), axis=0, keepdims=True)
    dest_o[0] = (start + rank_ref[0].astype(jnp.float32)).astype(jnp.int32)


def _dest(pstarts, cls3, rank3):
    nt, _, tm = cls3.shape
    idx = pl.BlockSpec((1, 1, tm), lambda i: (i, 0, 0))
    return pl.pallas_call(
        _dest_kernel,
        grid=(nt,),
        in_specs=[idx, idx, pl.BlockSpec((N_CLASSES, 1), lambda i: (0, 0))],
        out_specs=idx,
        out_shape=jax.ShapeDtypeStruct(cls3.shape, jnp.int32),
        compiler_params=_cparams(("arbitrary",)),
        name="dest",
    )(cls3, rank3, pstarts.astype(jnp.float32).reshape(N_CLASSES, 1))


def _for_rows(tm, fn):
    def trip(g, c):
        for u in range(ROW_DMA_UNROLL):
            fn(g * ROW_DMA_UNROLL + u, u)
        return c

    lax.fori_loop(0, tm // ROW_DMA_UNROLL, trip, 0)


def _dispatch_kernel(tm, pstart_ref, pend_ref, dest_ref, pdest_ref, h2_ref, w_ref, rows_out,
                     tile_ref, zero_ref, sems):
    i = pl.program_id(0)
    slot = i % 2

    def row_copy(d_ref, s, r):
        return pltpu.make_async_copy(tile_ref.at[s, pl.ds(r, 1)], rows_out.at[pl.ds(d_ref[0, 0, r], 1)], sems.at[s])

    @pl.when(i == 0)
    def _():
        zero_ref[...] = jnp.zeros_like(zero_ref)

        def zero_copy(c):
            start = pl.multiple_of(pend_ref[c] - EXPERT_BLOCK, EXPERT_BLOCK)
            return pltpu.make_async_copy(zero_ref, rows_out.at[pl.ds(start, EXPERT_BLOCK)], sems.at[2])

        def for_spans(fn):
            def trip(c, carry):
                @pl.when(pend_ref[c] > pstart_ref[c])
                def _():
                    fn(c)
                return carry

            lax.fori_loop(0, N_CLASSES, trip, 0)

        def tail_copy(b):
            start = pl.multiple_of(b * EXPERT_BLOCK, EXPERT_BLOCK)
            return pltpu.make_async_copy(zero_ref, rows_out.at[pl.ds(start, EXPERT_BLOCK)], sems.at[2])

        def for_tail(fn):
            def trip(b, carry):
                fn(b)
                return carry

            lax.fori_loop(pend_ref[N_CLASSES - 1] // EXPERT_BLOCK, rows_out.shape[0] // EXPERT_BLOCK, trip, 0)

        for_spans(lambda c: zero_copy(c).start())
        for_tail(lambda b: tail_copy(b).start())
        for_spans(lambda c: zero_copy(c).wait())
        for_tail(lambda b: tail_copy(b).wait())

    tile_ref[slot, :, 0:D_MODEL] = h2_ref[...]
    tile_ref[slot, :, D_MODEL:ROW_W] = w_ref[...]
    _for_rows(tm, lambda r, u: row_copy(dest_ref, slot, r).start(priority=u % 2))

    @pl.when(i > 0)
    def _():
        _for_rows(tm, lambda r, u: row_copy(pdest_ref, 1 - slot, r).wait())

    @pl.when(i == pl.num_programs(0) - 1)
    def _():
        _for_rows(tm, lambda r, u: row_copy(dest_ref, slot, r).wait())


def _dispatch(pstarts, pends, dest3, h2, wslab, n_rows, tm):
    n_tok = h2.shape[0]
    idx = lambda f: pl.BlockSpec((1, 1, tm), f, memory_space=pltpu.SMEM)
    grid_spec = pltpu.PrefetchScalarGridSpec(
        num_scalar_prefetch=2,
        grid=(n_tok // tm,),
        in_specs=[idx(lambda i, ps, pe: (i, 0, 0)), idx(lambda i, ps, pe: (jnp.maximum(i - 1, 0), 0, 0)),
                  pl.BlockSpec((tm, D_MODEL), lambda i, ps, pe: (i, 0)),
                  pl.BlockSpec((tm, LANES), lambda i, ps, pe: (i, 0))],
        out_specs=pl.BlockSpec(memory_space=pl.ANY),
        scratch_shapes=[pltpu.VMEM((2, tm, ROW_W), jnp.float32), pltpu.VMEM((EXPERT_BLOCK, ROW_W), jnp.float32),
                        pltpu.SemaphoreType.DMA((3,))],
    )
    return pl.pallas_call(
        functools.partial(_dispatch_kernel, tm),
        grid_spec=grid_spec,
        out_shape=jax.ShapeDtypeStruct((n_rows, ROW_W), jnp.float32),
        compiler_params=_cparams(("arbitrary",)),
        name="dispatch",
    )(pstarts, pends, dest3, dest3, h2, wslab)


def _combine_kernel(tm, dest_ref, ndest_ref, x1_ref, y_ref, o_ref, buf_ref, sems):
    i = pl.program_id(0)
    slot = i % 2

    def row_copy(d_ref, s, r):
        return pltpu.make_async_copy(y_ref.at[pl.ds(d_ref[0, 0, r], 1)], buf_ref.at[s, pl.ds(r, 1)], sems.at[s])

    @pl.when(i == 0)
    def _():
        _for_rows(tm, lambda r, u: row_copy(dest_ref, slot, r).start(priority=u % 2))

    @pl.when(i < pl.num_programs(0) - 1)
    def _():
        _for_rows(tm, lambda r, u: row_copy(ndest_ref, 1 - slot, r).start(priority=u % 2))

    _for_rows(tm, lambda r, u: row_copy(dest_ref, slot, r).wait())
    o_ref[...] = x1_ref[...] + buf_ref[slot]


def _combine(dest3, x1, y, tm):
    n_tok = x1.shape[0]
    nt = n_tok // tm
    idx = lambda f: pl.BlockSpec((1, 1, tm), f, memory_space=pltpu.SMEM)
    return pl.pallas_call(
        functools.partial(_combine_kernel, tm),
        grid=(nt,),
        in_specs=[idx(lambda i: (i, 0, 0)), idx(lambda i: (jnp.minimum(i + 1, nt - 1), 0, 0)),
                  pl.BlockSpec((tm, D_MODEL), lambda i: (i, 0)),
                  pl.BlockSpec(memory_space=pl.ANY)],
        out_specs=pl.BlockSpec((tm, D_MODEL), lambda i: (i, 0)),
        out_shape=jax.ShapeDtypeStruct(x1.shape, jnp.float32),
        scratch_shapes=[pltpu.VMEM((2, tm, D_MODEL), jnp.float32), pltpu.SemaphoreType.DMA((2,))],
        compiler_params=_cparams(("arbitrary",)),
        name="combine",
    )(dest3, dest3, x1, y)


EXPERT_STEP_BLOCKS = 2


def _expert_kernel(ea_ref, eb_ref, nused_ref, rows_ref, *refs):
    y_ref = refs[-1]
    step = pl.program_id(0)

    @pl.when(step * EXPERT_STEP_BLOCKS < nused_ref[0])
    def _():
        for j in range(EXPERT_STEP_BLOCKS):
            ua_ref, ub_ref, da_ref, db_ref = refs[4 * j:4 * j + 4]
            rows = slice(j * EXPERT_BLOCK, (j + 1) * EXPERT_BLOCK)
            h = rows_ref[rows, 0:D_MODEL].astype(jnp.bfloat16)

            def expert(up_ref, down_ref, w):
                u = _dot(h, up_ref[0])
                act = (jax.nn.silu(u[:, 0:EXPERT_FF]) * u[:, EXPERT_FF:]).astype(jnp.bfloat16)
                return w * _dot(act, down_ref[0])

            y_ref[rows, :] = (expert(ua_ref, da_ref, rows_ref[rows, D_MODEL:D_MODEL + 1])
                              + expert(ub_ref, db_ref, rows_ref[rows, D_MODEL + 1:D_MODEL + 2]))

    @pl.when(step * EXPERT_STEP_BLOCKS >= nused_ref[0])
    def _():
        y_ref[...] = jnp.zeros_like(y_ref)


def _experts(blk_ea, blk_eb, nused, rows, w_up, w_down):
    n_blocks = rows.shape[0] // EXPERT_BLOCK
    nsb = EXPERT_STEP_BLOCKS
    assert n_blocks % nsb == 0
    step_rows = nsb * EXPERT_BLOCK
    last_step = lambda nu: (nu[0] - 1) // nsb
    weight_specs = []
    for j in range(nsb):
        weight_specs += [
            pl.BlockSpec((1, D_MODEL, 2 * EXPERT_FF), lambda s, ea, eb, nu, j=j: (ea[nsb * s + j], 0, 0)),
            pl.BlockSpec((1, D_MODEL, 2 * EXPERT_FF), lambda s, ea, eb, nu, j=j: (eb[nsb * s + j], 0, 0)),
            pl.BlockSpec((1, EXPERT_FF, D_MODEL), lambda s, ea, eb, nu, j=j: (ea[nsb * s + j], 0, 0)),
            pl.BlockSpec((1, EXPERT_FF, D_MODEL), lambda s, ea, eb, nu, j=j: (eb[nsb * s + j], 0, 0)),
        ]
    grid_spec = pltpu.PrefetchScalarGridSpec(
        num_scalar_prefetch=3,
        grid=(n_blocks // nsb,),
        in_specs=[pl.BlockSpec((step_rows, ROW_W), lambda s, ea, eb, nu: (jnp.minimum(s, last_step(nu)), 0))]
        + weight_specs,
        out_specs=pl.BlockSpec((step_rows, D_MODEL), lambda s, ea, eb, nu: (s, 0)),
    )
    weights = [w_up, w_up, w_down, w_down] * nsb
    return pl.pallas_call(
        _expert_kernel,
        grid_spec=grid_spec,
        out_shape=jax.ShapeDtypeStruct((rows.shape[0], D_MODEL), jnp.float32),
        compiler_params=_cparams(("arbitrary",)),
        name="experts",
    )(blk_ea, blk_eb, nused, rows, *weights)


def _pick_tile(n, pref):
    t = min(n, pref)
    while n % t:
        t //= 2
    return t


def kernel(x, mem, positions, attn_norm_g, w_in, diff_qnorm_g, diff_knorm_g, diff_lambda, diff_subln_g,
           fox_qnorm_g, fox_knorm_g, fox_forget_b, mem_norm_g, w_mem_kv, mem_qnorm_g, mem_knorm_g, w_o_diff,
           w_o_fox, w_o_mem, w_out, ffn_norm_g, w_router_group, b_router_group, w_router_expert,
           b_router_expert, w_up, w_down):
    n_batch, seq, _ = x.shape
    mem_len = mem.shape[1]
    n_tok = n_batch * seq
    bf = jnp.bfloat16
    tm = _pick_tile(seq, 512)
    tq = _pick_tile(seq, 512)
    tk = _pick_tile(tq, 512)
    tiles_per_seq = seq // tm
    x2 = x.reshape(n_tok, D_MODEL)

    wi = w_in[0]
    o = 0
    parts = {}
    for name, width in (("qa", HW), ("ka", HW), ("va", HW), ("qb", HW), ("kb", HW), ("vb", HW),
                        ("fb", FOX_HEADS), ("qc", HW), ("g", 3 * D_MODEL)):
        parts[name] = wi[:, o:o + width]
        o += width
    w_main = jnp.concatenate([parts[k] for k in ("qa", "ka", "va", "qb", "kb", "vb", "qc", "g")], axis=1).astype(bf)
    w_f = jnp.pad(parts["fb"], ((0, 0), (0, LANES - FOX_HEADS))).astype(bf)
    fb = jnp.pad(fox_forget_b[0], (0, LANES - FOX_HEADS)).reshape(1, LANES)
    tile_gain = lambda g, n: jnp.tile(g, n).reshape(1, -1)
    gains = (tile_gain(diff_qnorm_g[0], _NORM_CHUNK // HEAD_DIM), tile_gain(diff_knorm_g[0], _NORM_CHUNK // HEAD_DIM),
             tile_gain(fox_qnorm_g[0], _NORM_CHUNK // HEAD_DIM), tile_gain(fox_knorm_g[0], _NORM_CHUNK // HEAD_DIM),
             mem_qnorm_g[0].reshape(1, MEM_DIM))

    kmem, vmem = _mem_kv(mem.reshape(n_batch * mem_len, D_MODEL), mem_norm_g[0].reshape(1, D_MODEL),
                         w_mem_kv[0].astype(bf), mem_knorm_g[0].reshape(1, MEM_DIM), n_batch, mem_len)

    qa, ka, va, qb, kb, vb, qc, gate, cum = _in_proj(
        x2, attn_norm_g[0].reshape(1, D_MODEL), w_main, w_f, fb, positions.reshape(n_tok, 1).astype(jnp.int32),
        gains, tm, tiles_per_seq)

    cum_t = cum[:, :FOX_HEADS].reshape(n_batch, seq, FOX_HEADS // 2, 2).transpose(0, 2, 3, 1)
    yb = _fox_attn(qb, kb, vb, cum_t, n_batch, seq, tq, tk)
    ya = _diff_attn(qa, ka, va, diff_lambda[0], diff_subln_g[0].reshape(1, LANES), n_batch, seq, tq, tk)

    w_r = jnp.concatenate([w_router_group[0], w_router_expert[0]], axis=1)
    w_r = jnp.pad(w_r, ((0, 0), (0, LANES - w_r.shape[1])))
    r_hi = w_r.astype(bf)
    r_lo = (w_r - r_hi.astype(jnp.float32)).astype(bf)
    r_b = jnp.pad(jnp.concatenate([b_router_group[0], b_router_expert[0]]), (0, LANES - N_GROUPS - N_EXPERTS))
    x1, h2, wslab, cls3, rank3, counts = _merge(x2, ya, yb, qc, kmem, vmem, gate, w_o_diff[0].astype(bf), w_o_fox[0].astype(bf),
                            w_o_mem[0].astype(bf), w_out[0].astype(bf), ffn_norm_g[0].reshape(1, D_MODEL),
                            r_hi, r_lo, r_b.reshape(1, LANES), tm, tiles_per_seq, mem_len)

    blk = EXPERT_BLOCK
    n_blocks = -(-(n_tok + N_LIVE_CLASSES * (blk - 1)) // (blk * EXPERT_STEP_BLOCKS)) * EXPERT_STEP_BLOCKS
    counts = counts[0]
    padded = (counts + blk - 1) // blk * blk
    pends = jnp.cumsum(padded)
    pstarts = pends - padded
    blk_start = jnp.arange(n_blocks, dtype=jnp.int32) * blk
    blk_cls = jnp.minimum(jnp.sum((pends[None, :] <= blk_start[:, None]).astype(jnp.int32), axis=1),
                          N_CLASSES - 1).astype(jnp.int32)
    per_group = EXPERTS_PER_GROUP * EXPERTS_PER_GROUP
    blk_ea = blk_cls // EXPERTS_PER_GROUP
    blk_eb = (blk_cls // per_group) * EXPERTS_PER_GROUP + blk_cls % EXPERTS_PER_GROUP
    nused = (pends[-1] // blk).astype(jnp.int32).reshape(1)
    pstarts = pstarts.astype(jnp.int32)

    dest3 = _dest(pstarts, cls3, rank3)
    rows = _dispatch(pstarts, pends.astype(jnp.int32), dest3, h2, wslab, n_blocks * blk, tm)
    y = _experts(blk_ea, blk_eb, nused, rows, w_up[0].astype(bf), w_down[0].astype(bf))
    out = _combine(dest3, x1, y, tm)
    return out.reshape(x.shape)
```

```python
import functools
import math

import jax
import jax.numpy as jnp
from jax import lax
from jax.experimental import pallas as pl
from jax.experimental.pallas import tpu as pltpu

D_MODEL = 1024
HEAD_DIM = 64
DIFF_HEADS = 4
FOX_HEADS = 8
MEM_HEADS = 4
MEM_DIM = 128
ROPE_THETA = 500000.0
ROPE_DIM = HEAD_DIM // 4
N_GROUPS = 4
EXPERTS_PER_GROUP = 8
N_EXPERTS = N_GROUPS * EXPERTS_PER_GROUP
EXPERT_FF = 512
EPS = 1e-6
NEG_INF = -1e30
LAMBDA_INIT = 0.8 - 0.6 * math.exp(-0.3 * 0)
LOG2E = 1.4426950408889634

LANES = 128
VMEM_LIMIT_BYTES = 56 * 1024 * 1024

HW = 512
N_CLASSES = 256
N_LIVE_CLASSES = N_GROUPS * (EXPERTS_PER_GROUP * (EXPERTS_PER_GROUP - 1) // 2)
ROW_W = D_MODEL + LANES
EXPERT_BLOCK = 256


def _cparams(sem):
    return pltpu.CompilerParams(dimension_semantics=sem, vmem_limit_bytes=VMEM_LIMIT_BYTES)


def _rms_rows(x, g):
    return x * lax.rsqrt(jnp.mean(x * x, axis=-1, keepdims=True) + EPS) * g


def _dot(a, b):
    return jnp.dot(a, b, preferred_element_type=jnp.float32)


def _dot_nt(a, b):
    return lax.dot_general(a, b, (((1,), (1,)), ((), ())), preferred_element_type=jnp.float32)


def _mem_kv_kernel(mem_ref, g_ref, w_ref, kg_ref, k_out, v_out):
    mn = _rms_rows(mem_ref[...], g_ref[...]).astype(jnp.bfloat16)
    kv = _dot(mn, w_ref[...])
    for h in range(MEM_HEADS):
        kh = kv[:, h * MEM_DIM:(h + 1) * MEM_DIM]
        k_out[:, h * MEM_DIM:(h + 1) * MEM_DIM] = _rms_rows(kh, kg_ref[...]).astype(jnp.bfloat16)
    v_out[...] = kv[:, HW:].astype(jnp.bfloat16)


def _mem_kv(mem2, g, w_kv, kg, n_batch, mem_len):
    return pl.pallas_call(
        _mem_kv_kernel,
        grid=(n_batch,),
        in_specs=[
            pl.BlockSpec((mem_len, D_MODEL), lambda b: (b, 0)),
            pl.BlockSpec((1, D_MODEL), lambda b: (0, 0)),
            pl.BlockSpec((D_MODEL, 2 * HW), lambda b: (0, 0)),
            pl.BlockSpec((1, MEM_DIM), lambda b: (0, 0)),
        ],
        out_specs=[
            pl.BlockSpec((mem_len, HW), lambda b: (b, 0)),
            pl.BlockSpec((mem_len, HW), lambda b: (b, 0)),
        ],
        out_shape=[jax.ShapeDtypeStruct((n_batch * mem_len, HW), jnp.bfloat16)] * 2,
        compiler_params=_cparams(("arbitrary",)),
        name="mem_kv",
    )(mem2, g, w_kv, kg)


_C_QA, _C_KA, _C_VA, _C_QB, _C_KB, _C_VB, _C_QC, _C_G = (i * HW for i in range(8))
_MAIN_COLS = 7 * HW + 3 * D_MODEL
_NORM_CHUNK = 256
_HEADS_PER_CHUNK = _NORM_CHUNK // HEAD_DIM
FOX_SLAB = FOX_HEADS * LANES
BIAS_PARTS = 3


def _group_rms(r, bd, g):
    ss = _dot((r * r).astype(jnp.bfloat16), bd)
    return r * lax.rsqrt(ss * (1.0 / HEAD_DIM) + EPS) * g


def _in_proj_kernel(tiles_per_seq, x_ref, g_ref, w_ref, wf_ref, fb_ref, pos_ref, rinv_ref, ra_ref, rb_ref, bd_ref, tri_ref,
                    place_ref, gqa_ref, gka_ref, gqb_ref, gkb_ref, gqc_ref,
                    qa_o, ka_o, va_o, qb_o, kb_o, vb_o, qc_o, gate_o, cum_o, carry_ref):
    i = pl.program_id(0)

    @pl.when(i == 0)
    def _():
        carry_ref[...] = jnp.zeros_like(carry_ref)

    h = _rms_rows(x_ref[...], g_ref[...]).astype(jnp.bfloat16)
    bd = bd_ref[...]
    nchunk = HW // _NORM_CHUNK
    lane = lax.broadcasted_iota(jnp.int32, (h.shape[0], LANES), 1)

    def split3(v):
        p1 = v.astype(jnp.bfloat16)
        r1 = v - p1.astype(jnp.float32)
        p2 = r1.astype(jnp.bfloat16)
        return p1, p2, (r1 - p2.astype(jnp.float32)).astype(jnp.bfloat16)

    ang = pos_ref[...].astype(jnp.float32) * rinv_ref[...]
    sin = jnp.sin(ang)
    rep = _NORM_CHUNK // LANES
    rope_c = jnp.concatenate([jnp.cos(ang)] * rep, axis=1)
    rope_a = jnp.concatenate([sin * ra_ref[...]] * rep, axis=1)
    rope_b = jnp.concatenate([sin * rb_ref[...]] * rep, axis=1)
    half = ROPE_DIM // 2

    def rope(y):
        return (y * rope_c + pltpu.roll(y, _NORM_CHUNK - half, axis=1) * rope_a
                + pltpu.roll(y, half, axis=1) * rope_b)

    def normed_chunk(r, c, gain, use_rope, scale):
        y = _group_rms(r[:, c * _NORM_CHUNK:(c + 1) * _NORM_CHUNK], bd, gain)
        if use_rope:
            y = rope(y)
        return y if scale is None else y * scale

    def packed_branch(col0, out, gain_ref, scale):
        gain = gain_ref[...]
        r = _dot(h, w_ref[:, col0:col0 + HW])
        for c in range(nchunk):
            out[:, c * _NORM_CHUNK:(c + 1) * _NORM_CHUNK] = normed_chunk(r, c, gain, True, scale).astype(jnp.bfloat16)

    def slab_branch(col0, out, gain_ref, scale, spare):
        gain = gain_ref[...]
        r = _dot(h, w_ref[:, col0:col0 + HW])
        for c in range(nchunk):
            y = normed_chunk(r, c, gain, False, scale)
            for hh in range(_HEADS_PER_CHUNK):
                src = y[:, (hh // 2) * LANES:(hh // 2 + 1) * LANES]
                if hh % 2:
                    src = pltpu.roll(src, HEAD_DIM, axis=1)
                head = c * _HEADS_PER_CHUNK + hh
                out[:, head * LANES:(head + 1) * LANES] = jnp.where(lane < HEAD_DIM, src, spare(head)).astype(jnp.bfloat16)

    q_scale = HEAD_DIM ** -0.5 * LOG2E
    slab_branch(_C_QB, qb_o, gqb_ref, q_scale, lambda head: jnp.where(
        (lane >= HEAD_DIM) & (lane < HEAD_DIM + BIAS_PARTS), 1.0, 0.0))

    z = _dot(h, wf_ref[...]) + fb_ref[...]
    logf = -(jnp.maximum(-z, 0.0) + jnp.log1p(jnp.exp(-jnp.abs(z))))
    va_o[...] = _dot(h, w_ref[:, _C_VA:_C_VA + HW]).astype(jnp.bfloat16)
    vb_o[...] = _dot(h, w_ref[:, _C_VB:_C_VB + HW]).astype(jnp.bfloat16)

    tri = tri_ref[...]
    carry = jnp.where(i % tiles_per_seq == 0, 0.0, carry_ref[...])
    c3 = _dot(tri, jnp.concatenate(split3(logf), axis=1))
    cum = (c3[:, 0:LANES] + c3[:, LANES:2 * LANES]) + c3[:, 2 * LANES:3 * LANES] + carry
    cum_o[...] = cum.T[0:FOX_HEADS]
    carry_ref[...] = cum[cum.shape[0] - 1:, :]

    gqc = gqc_ref[...]
    rc = _dot(h, w_ref[:, _C_QC:_C_QC + HW])
    for hd in range(MEM_HEADS):
        sl = slice(hd * MEM_DIM, (hd + 1) * MEM_DIM)
        qc_o[:, sl] = (_rms_rows(rc[:, sl], gqc) * MEM_DIM ** -0.5).astype(jnp.bfloat16)

    packed_branch(_C_QA, qa_o, gqa_ref, q_scale)

    pieces = [p.astype(jnp.float32) for p in split3(jnp.where(lane < FOX_HEADS, cum * (-LOG2E), 0.0))]
    packed = pieces[0]
    for k in range(1, BIAS_PARTS):
        packed = packed + pltpu.roll(pieces[k], k * FOX_HEADS, axis=1)
    kbias = _dot(packed.astype(jnp.bfloat16), place_ref[...])
    packed_branch(_C_KA, ka_o, gka_ref, None)
    slab_branch(_C_KB, kb_o, gkb_ref, None, lambda head: kbias[:, head * LANES:(head + 1) * LANES])

    for c in range(3 * D_MODEL // HW):
        lo = _C_G + c * HW
        r = _dot(h, w_ref[:, lo:lo + HW])
        gate_o[:, c * HW:(c + 1) * HW] = jax.nn.sigmoid(r).astype(jnp.bfloat16)


def _in_proj(x2, g, w_main, w_f, fb, pos, gains, tm, tiles_per_seq):
    n_tok = x2.shape[0]
    half = ROPE_DIM // 2
    head_lane = jnp.arange(LANES, dtype=jnp.int32) % HEAD_DIM
    inv = ROPE_THETA ** (-jnp.arange(0, ROPE_DIM, 2, dtype=jnp.float32) / ROPE_DIM)
    rope_inv = jnp.where(head_lane < ROPE_DIM, inv[head_lane % half], 0.0).reshape(1, LANES)
    rope_sa = jnp.where(head_lane < half, -1.0, 0.0).reshape(1, LANES)
    rope_sb = jnp.where((head_lane >= half) & (head_lane < ROPE_DIM), 1.0, 0.0).reshape(1, LANES)
    bd = (lax.broadcasted_iota(jnp.int32, (_NORM_CHUNK, _NORM_CHUNK), 0) // HEAD_DIM
          == lax.broadcasted_iota(jnp.int32, (_NORM_CHUNK, _NORM_CHUNK), 1) // HEAD_DIM).astype(jnp.bfloat16)
    tri = (lax.broadcasted_iota(jnp.int32, (tm, tm), 1)
           <= lax.broadcasted_iota(jnp.int32, (tm, tm), 0)).astype(jnp.bfloat16)
    src = lax.broadcasted_iota(jnp.int32, (LANES, FOX_SLAB), 0)
    dst = lax.broadcasted_iota(jnp.int32, (LANES, FOX_SLAB), 1)
    place = ((src < BIAS_PARTS * FOX_HEADS)
             & (dst == (src % FOX_HEADS) * LANES + HEAD_DIM + src // FOX_HEADS)).astype(jnp.bfloat16)
    const = lambda shape: pl.BlockSpec(shape, lambda i: (0, 0))
    resident = lambda shape: pl.BlockSpec(shape, lambda i: (0, 0), pipeline_mode=pl.Buffered(1))
    row = lambda w: pl.BlockSpec((tm, w), lambda i: (i, 0))
    slab = lambda w: jax.ShapeDtypeStruct((n_tok, w), jnp.bfloat16)
    return pl.pallas_call(
        functools.partial(_in_proj_kernel, tiles_per_seq),
        grid=(n_tok // tm,),
        in_specs=[row(D_MODEL), const((1, D_MODEL)), resident((D_MODEL, _MAIN_COLS)), const((D_MODEL, LANES)),
                  const((1, LANES)), row(1), const((1, LANES)), const((1, LANES)), const((1, LANES)),
                  const((_NORM_CHUNK, _NORM_CHUNK)), const((tm, tm)), const((LANES, FOX_SLAB)),
                  const((1, _NORM_CHUNK)), const((1, _NORM_CHUNK)), const((1, _NORM_CHUNK)),
                  const((1, _NORM_CHUNK)), const((1, MEM_DIM))],
        out_specs=[row(HW), row(HW), row(HW), row(FOX_SLAB), row(FOX_SLAB), row(HW), row(HW),
                   row(3 * D_MODEL), pl.BlockSpec((FOX_HEADS, tm), lambda i: (0, i))],
        out_shape=[slab(HW), slab(HW), slab(HW), slab(FOX_SLAB), slab(FOX_SLAB), slab(HW), slab(HW),
                   slab(3 * D_MODEL), jax.ShapeDtypeStruct((FOX_HEADS, n_tok), jnp.float32)],
        scratch_shapes=[pltpu.VMEM((1, LANES), jnp.float32)],
        compiler_params=_cparams(("arbitrary",)),
        name="in_proj",
    )(x2, g, w_main, w_f, fb, pos, rope_inv, rope_sa, rope_sb, bd, tri, place, *gains)


FOX_V_ROWS = 80
DIFF_V_ROWS = 144


Q_SPLIT = 2


def _flash_transposed(qs, k_of, v_of, shifts, tq, v_rows, finish):
    qi = pl.program_id(2)
    tqp = tq // Q_SPLIT
    parts = [(w, h) for w in range(len(qs)) for h in range(Q_SPLIT)]
    q_parts = [qs[w][h * tqp:(h + 1) * tqp] for (w, h) in parts]
    shift_parts = [0.0 if shifts[w] is None else shifts[w][:, h * tqp:(h + 1) * tqp] for (w, h) in parts]

    def block(k0, n, diag, state):
        def n_keys(h):
            return n if diag is None else min(n, diag + (h + 1) * tqp)

        sc = []
        for c, (w, h) in enumerate(parts):
            s = _dot_nt(k_of(w, k0, n_keys(h)), q_parts[c])
            if diag is not None:
                key = lax.broadcasted_iota(jnp.int32, s.shape, 0) - diag
                qry = lax.broadcasted_iota(jnp.int32, s.shape, 1) + h * tqp
                s = jnp.where(key <= qry, s, NEG_INF)
            sc.append((s, jnp.max(s, axis=0, keepdims=True)))
        out = []
        for c, (w, h) in enumerate(parts):
            (s, s_max), (m_old, acc_old), shift = sc[c], state[c], shift_parts[c]
            m_new = jnp.maximum(m_old, s_max + shift)
            p = jnp.exp2(s - (m_new - shift))
            acc_new = jnp.exp2(m_old - m_new) * acc_old + _dot(v_of(w, k0, n_keys(h)), p.astype(jnp.bfloat16))
            out.append((m_new, acc_new))
        return tuple(out)

    def done(state):
        finish([jnp.concatenate([state[w * Q_SPLIT + h][1] for h in range(Q_SPLIT)], axis=1)
                for w in range(len(qs))])

    init = tuple((jnp.full((1, tqp), NEG_INF, jnp.float32), jnp.zeros((v_rows, tqp), jnp.float32)) for _ in parts)
    state = lax.fori_loop(0, qi // 2, lambda j, st: block(pl.multiple_of(j * 2 * tq, 2 * tq), 2 * tq, None, st), init)

    @pl.when(qi % 2 == 1)
    def _():
        done(block(pl.multiple_of((qi - 1) * tq, tq), 2 * tq, tq, state))

    @pl.when(qi % 2 == 0)
    def _():
        done(block(pl.multiple_of(qi * tq, tq), tq, 0, state))


ATTN_CHAINS = 4


def _fill_values_t(v_ref, vt_ref, head_w, tk):
    n_heads, v_rows, seq = vt_ref.shape
    tail = lax.broadcasted_iota(jnp.int32, (v_rows - head_w, seq), 0)
    ones_row = jnp.where(tail == 0, 1.0, 0.0).astype(jnp.bfloat16)
    for h in range(n_heads):
        vt_ref[h, head_w:v_rows, :] = ones_row
    for c in range(seq // tk):
        blk_t = v_ref[c * tk:(c + 1) * tk, :].astype(jnp.float32).T
        for h in range(n_heads):
            vt_ref[h, 0:head_w, c * tk:(c + 1) * tk] = blk_t[h * head_w:(h + 1) * head_w].astype(jnp.bfloat16)

FOX_STEP_HEADS = ATTN_CHAINS
DIFF_STEP_HEADS = ATTN_CHAINS // 2


def _fox_kernel(tq, tk, q_ref, k_ref, v_ref, cq_ref, o_ref, vt_ref):
    heads = range(FOX_STEP_HEADS)

    @pl.when(pl.program_id(2) == 0)
    def _():
        _fill_values_t(v_ref, vt_ref, HEAD_DIM, tk)

    qs = [q_ref[:, w * LANES:(w + 1) * LANES] for w in heads]
    head0 = pl.program_id(1) * FOX_STEP_HEADS
    shifts = [cq_ref[pl.ds(head0 + w, 1), :] * LOG2E for w in heads]

    def finish(accs):
        o_t = jnp.concatenate([a[0:HEAD_DIM] / a[HEAD_DIM:HEAD_DIM + 1] for a in accs], axis=0)
        o_ref[...] = o_t.T.astype(jnp.bfloat16)

    _flash_transposed(qs, lambda w, k0, n: k_ref[pl.ds(k0, n), w * LANES:(w + 1) * LANES],
                      lambda w, k0, n: vt_ref[w, :, pl.ds(k0, n)], shifts, tq, FOX_V_ROWS, finish)


def _fox_attn(qb, kb, vb, cum_t, n_batch, seq, tq, tk):
    nq = seq // tq
    hs = FOX_STEP_HEADS
    return pl.pallas_call(
        functools.partial(_fox_kernel, tq, tk),
        grid=(n_batch, FOX_HEADS // hs, nq),
        in_specs=[
            pl.BlockSpec((tq, hs * LANES), lambda b, j, i: (b * nq + i, j)),
            pl.BlockSpec((seq, hs * LANES), lambda b, j, i: (b, j)),
            pl.BlockSpec((seq, hs * HEAD_DIM), lambda b, j, i: (b, j)),
            pl.BlockSpec((FOX_HEADS, tq), lambda b, j, i: (0, b * nq + i)),
        ],
        out_specs=pl.BlockSpec((tq, hs * HEAD_DIM), lambda b, j, i: (b * nq + i, j)),
        out_shape=jax.ShapeDtypeStruct((n_batch * seq, HW), jnp.bfloat16),
        scratch_shapes=[pltpu.VMEM((hs, FOX_V_ROWS, seq), jnp.bfloat16)],
        compiler_params=_cparams(("arbitrary", "arbitrary", "arbitrary")),
        name="fox_attn",
    )(qb, kb, vb, cum_t)


def _diff_kernel(tq, tk, q_ref, k_ref, v_ref, lam_ref, g_ref, o_ref, vt_ref):
    @pl.when(pl.program_id(2) == 0)
    def _():
        _fill_values_t(v_ref, vt_ref, 2 * HEAD_DIM, tk)

    lp = lam_ref[...]
    lam = (jnp.exp(jnp.sum(lp[0:1] * lp[1:2], axis=1, keepdims=True))
           - jnp.exp(jnp.sum(lp[2:3] * lp[3:4], axis=1, keepdims=True)) + LAMBDA_INIT)
    lane = lax.broadcasted_iota(jnp.int32, (tq, LANES), 1)
    qs = []
    for hd in range(DIFF_STEP_HEADS):
        q = q_ref[:, hd * LANES:(hd + 1) * LANES]
        zero = jnp.zeros_like(q)
        qs += [jnp.where(lane < HEAD_DIM, q, zero), jnp.where(lane >= HEAD_DIM, q, zero)]
    dv = 2 * HEAD_DIM

    def finish(accs):
        outs = []
        for hd in range(DIFF_STEP_HEADS):
            acc1, acc2 = accs[2 * hd], accs[2 * hd + 1]
            o_t = acc1[0:dv] / acc1[dv:dv + 1] - lam * (acc2[0:dv] / acc2[dv:dv + 1])
            outs.append(o_t * lax.rsqrt(jnp.mean(o_t * o_t, axis=0, keepdims=True) + EPS))
        gain = jnp.concatenate([g_ref[...]] * DIFF_STEP_HEADS, axis=1)
        o_ref[...] = (jnp.concatenate(outs, axis=0).T * gain * (1.0 - LAMBDA_INIT)).astype(jnp.bfloat16)

    _flash_transposed(qs, lambda w, k0, n: k_ref[pl.ds(k0, n), (w // 2) * LANES:(w // 2 + 1) * LANES],
                      lambda w, k0, n: vt_ref[w // 2, :, pl.ds(k0, n)], [None] * len(qs), tq, DIFF_V_ROWS, finish)


def _diff_attn(qa, ka, va, lam_params, subln_g, n_batch, seq, tq, tk):
    nq = seq // tq
    hs = DIFF_STEP_HEADS
    return pl.pallas_call(
        functools.partial(_diff_kernel, tq, tk),
        grid=(n_batch, DIFF_HEADS // hs, nq),
        in_specs=[
            pl.BlockSpec((tq, hs * LANES), lambda b, j, i: (b * nq + i, j)),
            pl.BlockSpec((seq, hs * LANES), lambda b, j, i: (b, j)),
            pl.BlockSpec((seq, hs * LANES), lambda b, j, i: (b, j)),
            pl.BlockSpec((4, HEAD_DIM), lambda b, j, i: (0, 0)),
            pl.BlockSpec((1, LANES), lambda b, j, i: (0, 0)),
        ],
        out_specs=pl.BlockSpec((tq, hs * LANES), lambda b, j, i: (b * nq + i, j)),
        out_shape=jax.ShapeDtypeStruct(qa.shape, jnp.bfloat16),
        scratch_shapes=[pltpu.VMEM((hs, DIFF_V_ROWS, seq), jnp.bfloat16)],
        compiler_params=_cparams(("arbitrary", "arbitrary", "arbitrary")),
        name="diff_attn",
    )(qa, ka, va, lam_params, subln_g)


def _merge_kernel(x_ref, ya_ref, yb_ref, qc_ref, km_ref, vm_ref, gate_ref, wd_ref, wf_ref, wm_ref, wo_ref,
                  fg_ref, rhi_ref, rlo_ref, rb_ref, tri_ref, x1_o, h2_o, w_o, cls_o, rank_o, cnt_o, carry_ref, lg_ref):
    i = pl.program_id(0)

    @pl.when(i == 0)
    def _():
        carry_ref[...] = jnp.zeros_like(carry_ref)
        lg_ref[...] = jnp.zeros_like(lg_ref)

    lg_prev = lg_ref[(i + 1) % 2]

    heads = []
    for h in range(MEM_HEADS):
        sl = slice(h * MEM_DIM, (h + 1) * MEM_DIM)
        s = _dot_nt(qc_ref[:, sl], km_ref[:, sl])
        p = jnp.exp(s - jnp.max(s, axis=1, keepdims=True))
        o = _dot(p.astype(jnp.bfloat16), vm_ref[:, sl]) / jnp.sum(p, axis=1, keepdims=True)
        heads.append(o.astype(jnp.bfloat16))
    yc = jnp.concatenate(heads, axis=1)

    merged = (gate_ref[:, 0:D_MODEL].astype(jnp.float32) * _dot(ya_ref[...], wd_ref[...])
              + gate_ref[:, D_MODEL:2 * D_MODEL].astype(jnp.float32) * _dot(yb_ref[...], wf_ref[...])
              + gate_ref[:, 2 * D_MODEL:3 * D_MODEL].astype(jnp.float32) * _dot(yc, wm_ref[...]))
    _route_tile(lg_prev, i > 0, tri_ref, w_o, cls_o, rank_o, cnt_o, carry_ref)
    x1 = x_ref[...] + _dot(merged.astype(jnp.bfloat16), wo_ref[...])
    x1_o[...] = x1
    h2 = _rms_rows(x1, fg_ref[...])
    h2_o[...] = h2
    a_hi = h2.astype(jnp.bfloat16)
    a_lo = (h2 - a_hi.astype(jnp.float32)).astype(jnp.bfloat16)
    lg_ref[i % 2] = (_dot(a_hi, rhi_ref[...]) + _dot(a_lo, rhi_ref[...]) + _dot(a_hi, rlo_ref[...])) + rb_ref[...]


def _merge(x2, ya, yb, qc, kmem, vmem, gate, wd, wf, wm, wo, fg, rhi, rlo, rb, tm, tiles_per_seq, mem_len):
    n_tok = x2.shape[0]
    nt = n_tok // tm
    cur = lambda i: jnp.minimum(i, nt - 1)
    prev = lambda i: jnp.maximum(i - 1, 0)
    const = lambda shape: pl.BlockSpec(shape, lambda i: (0, 0))
    row = lambda w: pl.BlockSpec((tm, w), lambda i: (cur(i), 0))
    memspec = pl.BlockSpec((mem_len, HW), lambda i: (cur(i) // tiles_per_seq, 0))
    tri = (lax.broadcasted_iota(jnp.int32, (tm, tm), 1)
           < lax.broadcasted_iota(jnp.int32, (tm, tm), 0)).astype(jnp.bfloat16)
    idx_row = pl.BlockSpec((1, 1, tm), lambda i: (prev(i), 0, 0))
    return pl.pallas_call(
        _merge_kernel,
        grid=(nt + 1,),
        in_specs=[row(D_MODEL), row(HW), row(HW), row(HW), memspec, memspec, row(3 * D_MODEL),
                  const((HW, D_MODEL)), const((HW, D_MODEL)), const((HW, D_MODEL)), const((D_MODEL, D_MODEL)),
                  const((1, D_MODEL)), const((D_MODEL, LANES)), const((D_MODEL, LANES)), const((1, LANES)),
                  const((tm, tm))],
        out_specs=[row(D_MODEL), row(D_MODEL), pl.BlockSpec((tm, LANES), lambda i: (prev(i), 0)), idx_row, idx_row,
                   const((1, N_CLASSES))],
        out_shape=[jax.ShapeDtypeStruct((n_tok, D_MODEL), jnp.float32),
                   jax.ShapeDtypeStruct((n_tok, D_MODEL), jnp.float32),
                   jax.ShapeDtypeStruct((n_tok, LANES), jnp.float32),
                   jax.ShapeDtypeStruct((nt, 1, tm), jnp.int32),
                   jax.ShapeDtypeStruct((nt, 1, tm), jnp.int32),
                   jax.ShapeDtypeStruct((1, N_CLASSES), jnp.int32)],
        scratch_shapes=[pltpu.VMEM((1, N_CLASSES), jnp.float32), pltpu.VMEM((2, tm, LANES), jnp.float32)],
        compiler_params=_cparams(("arbitrary",)),
        name="merge",
    )(x2, ya, yb, qc, kmem, vmem, gate, wd, wf, wm, wo, fg, rhi, rlo, rb, tri)


_EXPERT_LANE0 = N_GROUPS


def _route_tile(lg, counted, tri_ref, w_o, cls_o, rank_o, cnt_o, carry_ref):
    lanei = lax.broadcasted_iota(jnp.int32, lg.shape, 1)
    lane = lanei.astype(jnp.float32)
    ninf = jnp.float32(-jnp.inf)
    big = jnp.float32(LANES)

    def first_max(vals):
        top = jnp.max(vals, axis=1, keepdims=True)
        return top, jnp.min(jnp.where(vals == top, lane, big), axis=1, keepdims=True)

    is_group = lanei < N_GROUPS
    gmax, gidx = first_max(jnp.where(is_group, lg, ninf))
    gate = 1.0 / jnp.sum(jnp.where(is_group, jnp.exp(lg - gmax), 0.0), axis=1, keepdims=True)
    lo = _EXPERT_LANE0 + EXPERTS_PER_GROUP * gidx
    el = jnp.where((lane >= lo) & (lane < lo + EXPERTS_PER_GROUP), lg, ninf)
    t1, i1 = first_max(el)
    t2, i2 = first_max(jnp.where(lane == i1, ninf, el))
    e2 = jnp.exp(t2 - t1)
    w1 = gate * (1.0 / (1.0 + e2))
    w2 = gate * (e2 / (1.0 + e2))
    first_low = i1 < i2
    ea = jnp.minimum(i1, i2) - lo
    eb = jnp.maximum(i1, i2) - lo
    wa = jnp.where(first_low, w1, w2)
    wb = jnp.where(first_low, w2, w1)
    cls = (EXPERTS_PER_GROUP * EXPERTS_PER_GROUP) * gidx + EXPERTS_PER_GROUP * ea + eb

    cls_lane = lax.broadcasted_iota(jnp.int32, (lg.shape[0], N_CLASSES), 1).astype(jnp.float32)
    onehot = cls_lane == cls
    before = _dot(tri_ref[...], jnp.where(onehot, 1.0, 0.0).astype(jnp.bfloat16)) + carry_ref[...]
    rank = jnp.sum(jnp.where(onehot, before, 0.0), axis=1, keepdims=True)
    carry_ref[...] = carry_ref[...] + jnp.where(counted, jnp.sum(jnp.where(onehot, 1.0, 0.0), axis=0, keepdims=True), 0.0)
    cnt_o[...] = carry_ref[...].astype(jnp.int32)

    w_o[...] = jnp.where(lanei == 0, wa, jnp.where(lanei == 1, wb, 0.0))
    idx_t = jnp.where(lanei == 0, cls, jnp.where(lanei == 1, rank, 0.0)).T
    cls_o[0] = idx_t[0:1, :].astype(jnp.int32)
    rank_o[0] = idx_t[1:2, :].astype(jnp.int32)


ROW_DMA_UNROLL = 8


def _dest_kernel(cls_ref, rank_ref, ps_ref, dest_o):
    sub = lax.broadcasted_iota(jnp.int32, (N_CLASSES, cls_ref.shape[2]), 0).astype(jnp.float32)
    for t in range(cls_ref.shape[0]):
        cls = cls_ref[t].astype(jnp.float32)
        start = jnp.sum(jnp.where(sub == cls, ps_ref[...], 0.0), axis=0, keepdims=True)
        dest_o[t] = (start + rank_ref[t].astype(jnp.float32)).astype(jnp.int32)


def _dest(pstarts, cls3, rank3):
    nt, _, tm = cls3.shape
    tiles = _pick_tile(nt, 8)
    idx = pl.BlockSpec((tiles, 1, tm), lambda i: (i, 0, 0))
    return pl.pallas_call(
        _dest_kernel,
        grid=(nt // tiles,),
        in_specs=[idx, idx, pl.BlockSpec((N_CLASSES, 1), lambda i: (0, 0))],
        out_specs=idx,
        out_shape=jax.ShapeDtypeStruct(cls3.shape, jnp.int32),
        compiler_params=_cparams(("arbitrary",)),
        name="dest",
    )(cls3, rank3, pstarts.astype(jnp.float32).reshape(N_CLASSES, 1))


def _for_rows(tm, fn):
    def trip(g, c):
        for u in range(ROW_DMA_UNROLL):
            fn(g * ROW_DMA_UNROLL + u, u)
        return c

    lax.fori_loop(0, tm // ROW_DMA_UNROLL, trip, 0)


def _dispatch_kernel(tm, pstart_ref, pend_ref, dest_ref, pdest_ref, h2_ref, w_ref, rows_out,
                     tile_ref, zero_ref, sems):
    i = pl.program_id(0)
    slot = i % 2

    def row_copy(d_ref, s, r):
        return pltpu.make_async_copy(tile_ref.at[s, pl.ds(r, 1)], rows_out.at[pl.ds(d_ref[0, 0, r], 1)], sems.at[s])

    @pl.when(i == 0)
    def _():
        zero_ref[...] = jnp.zeros_like(zero_ref)

        def zero_copy(c):
            start = pl.multiple_of(pend_ref[c] - EXPERT_BLOCK, EXPERT_BLOCK)
            return pltpu.make_async_copy(zero_ref, rows_out.at[pl.ds(start, EXPERT_BLOCK)], sems.at[2])

        def for_spans(fn):
            def trip(c, carry):
                @pl.when(pend_ref[c] > pstart_ref[c])
                def _():
                    fn(c)
                return carry

            lax.fori_loop(0, N_CLASSES, trip, 0)

        def tail_copy(b):
            start = pl.multiple_of(b * EXPERT_BLOCK, EXPERT_BLOCK)
            return pltpu.make_async_copy(zero_ref, rows_out.at[pl.ds(start, EXPERT_BLOCK)], sems.at[2])

        def for_tail(fn):
            def trip(b, carry):
                fn(b)
                return carry

            lax.fori_loop(pend_ref[N_CLASSES - 1] // EXPERT_BLOCK, rows_out.shape[0] // EXPERT_BLOCK, trip, 0)

        for_spans(lambda c: zero_copy(c).start())
        for_tail(lambda b: tail_copy(b).start())
        for_spans(lambda c: zero_copy(c).wait())
        for_tail(lambda b: tail_copy(b).wait())

    tile_ref[slot, :, 0:D_MODEL] = h2_ref[...]
    tile_ref[slot, :, D_MODEL:ROW_W] = w_ref[...]
    _for_rows(tm, lambda r, u: row_copy(dest_ref, slot, r).start(priority=u % 2))

    @pl.when(i > 0)
    def _():
        _for_rows(tm, lambda r, u: row_copy(pdest_ref, 1 - slot, r).wait())

    @pl.when(i == pl.num_programs(0) - 1)
    def _():
        _for_rows(tm, lambda r, u: row_copy(dest_ref, slot, r).wait())


def _dispatch(pstarts, pends, dest3, h2, wslab, n_rows, tm):
    n_tok = h2.shape[0]
    idx = lambda f: pl.BlockSpec((1, 1, tm), f, memory_space=pltpu.SMEM)
    grid_spec = pltpu.PrefetchScalarGridSpec(
        num_scalar_prefetch=2,
        grid=(n_tok // tm,),
        in_specs=[idx(lambda i, ps, pe: (i, 0, 0)), idx(lambda i, ps, pe: (jnp.maximum(i - 1, 0), 0, 0)),
                  pl.BlockSpec((tm, D_MODEL), lambda i, ps, pe: (i, 0)),
                  pl.BlockSpec((tm, LANES), lambda i, ps, pe: (i, 0))],
        out_specs=pl.BlockSpec(memory_space=pl.ANY),
        scratch_shapes=[pltpu.VMEM((2, tm, ROW_W), jnp.float32), pltpu.VMEM((EXPERT_BLOCK, ROW_W), jnp.float32),
                        pltpu.SemaphoreType.DMA((3,))],
    )
    return pl.pallas_call(
        functools.partial(_dispatch_kernel, tm),
        grid_spec=grid_spec,
        out_shape=jax.ShapeDtypeStruct((n_rows, ROW_W), jnp.float32),
        compiler_params=_cparams(("arbitrary",)),
        name="dispatch",
    )(pstarts, pends, dest3, dest3, h2, wslab)


def _combine_kernel(tm, dest_ref, ndest_ref, x1_ref, y_ref, o_ref, buf_ref, sems):
    i = pl.program_id(0)
    slot = i % 2

    def row_copy(d_ref, s, r):
        return pltpu.make_async_copy(y_ref.at[pl.ds(d_ref[0, 0, r], 1)], buf_ref.at[s, pl.ds(r, 1)], sems.at[s])

    @pl.when(i == 0)
    def _():
        _for_rows(tm, lambda r, u: row_copy(dest_ref, slot, r).start(priority=u % 2))

    @pl.when(i < pl.num_programs(0) - 1)
    def _():
        _for_rows(tm, lambda r, u: row_copy(ndest_ref, 1 - slot, r).start(priority=u % 2))

    _for_rows(tm, lambda r, u: row_copy(dest_ref, slot, r).wait())
    o_ref[...] = x1_ref[...] + buf_ref[slot]


def _combine(dest3, x1, y, tm):
    n_tok = x1.shape[0]
    nt = n_tok // tm
    idx = lambda f: pl.BlockSpec((1, 1, tm), f, memory_space=pltpu.SMEM)
    return pl.pallas_call(
        functools.partial(_combine_kernel, tm),
        grid=(nt,),
        in_specs=[idx(lambda i: (i, 0, 0)), idx(lambda i: (jnp.minimum(i + 1, nt - 1), 0, 0)),
                  pl.BlockSpec((tm, D_MODEL), lambda i: (i, 0)),
                  pl.BlockSpec(memory_space=pl.ANY)],
        out_specs=pl.BlockSpec((tm, D_MODEL), lambda i: (i, 0)),
        out_shape=jax.ShapeDtypeStruct(x1.shape, jnp.float32),
        scratch_shapes=[pltpu.VMEM((2, tm, D_MODEL), jnp.float32), pltpu.SemaphoreType.DMA((2,))],
        compiler_params=_cparams(("arbitrary",)),
        name="combine",
    )(dest3, dest3, x1, y)


EXPERT_STEP_BLOCKS = 2


def _expert_kernel(ea_ref, eb_ref, nused_ref, rows_ref, *refs):
    y_ref = refs[-1]
    step = pl.program_id(0)

    @pl.when(step * EXPERT_STEP_BLOCKS < nused_ref[0])
    def _():
        for j in range(EXPERT_STEP_BLOCKS):
            ua_ref, ub_ref, da_ref, db_ref = refs[4 * j:4 * j + 4]
            rows = slice(j * EXPERT_BLOCK, (j + 1) * EXPERT_BLOCK)
            h = rows_ref[rows, 0:D_MODEL].astype(jnp.bfloat16)

            def expert(up_ref, down_ref, w):
                u = _dot(h, up_ref[0])
                act = (jax.nn.silu(u[:, 0:EXPERT_FF]) * u[:, EXPERT_FF:]).astype(jnp.bfloat16)
                return w * _dot(act, down_ref[0])

            y_ref[rows, :] = (expert(ua_ref, da_ref, rows_ref[rows, D_MODEL:D_MODEL + 1])
                              + expert(ub_ref, db_ref, rows_ref[rows, D_MODEL + 1:D_MODEL + 2]))

    @pl.when(step * EXPERT_STEP_BLOCKS >= nused_ref[0])
    def _():
        y_ref[...] = jnp.zeros_like(y_ref)


def _experts(blk_ea, blk_eb, nused, rows, w_up, w_down):
    n_blocks = rows.shape[0] // EXPERT_BLOCK
    nsb = EXPERT_STEP_BLOCKS
    assert n_blocks % nsb == 0
    step_rows = nsb * EXPERT_BLOCK
    last_step = lambda nu: (nu[0] - 1) // nsb
    weight_specs = []
    for j in range(nsb):
        weight_specs += [
            pl.BlockSpec((1, D_MODEL, 2 * EXPERT_FF), lambda s, ea, eb, nu, j=j: (ea[nsb * s + j], 0, 0)),
            pl.BlockSpec((1, D_MODEL, 2 * EXPERT_FF), lambda s, ea, eb, nu, j=j: (eb[nsb * s + j], 0, 0)),
            pl.BlockSpec((1, EXPERT_FF, D_MODEL), lambda s, ea, eb, nu, j=j: (ea[nsb * s + j], 0, 0)),
            pl.BlockSpec((1, EXPERT_FF, D_MODEL), lambda s, ea, eb, nu, j=j: (eb[nsb * s + j], 0, 0)),
        ]
    grid_spec = pltpu.PrefetchScalarGridSpec(
        num_scalar_prefetch=3,
        grid=(n_blocks // nsb,),
        in_specs=[pl.BlockSpec((step_rows, ROW_W), lambda s, ea, eb, nu: (jnp.minimum(s, last_step(nu)), 0))]
        + weight_specs,
        out_specs=pl.BlockSpec((step_rows, D_MODEL), lambda s, ea, eb, nu: (s, 0)),
    )
    weights = [w_up, w_up, w_down, w_down] * nsb
    return pl.pallas_call(
        _expert_kernel,
        grid_spec=grid_spec,
        out_shape=jax.ShapeDtypeStruct((rows.shape[0], D_MODEL), jnp.float32),
        compiler_params=_cparams(("arbitrary",)),
        name="experts",
    )(blk_ea, blk_eb, nused, rows, *weights)


def _pick_tile(n, pref):
    t = min(n, pref)
    while n % t:
        t //= 2
    return t


def kernel(x, mem, positions, attn_norm_g, w_in, diff_qnorm_g, diff_knorm_g, diff_lambda, diff_subln_g,
           fox_qnorm_g, fox_knorm_g, fox_forget_b, mem_norm_g, w_mem_kv, mem_qnorm_g, mem_knorm_g, w_o_diff,
           w_o_fox, w_o_mem, w_out, ffn_norm_g, w_router_group, b_router_group, w_router_expert,
           b_router_expert, w_up, w_down):
    n_batch, seq, _ = x.shape
    mem_len = mem.shape[1]
    n_tok = n_batch * seq
    bf = jnp.bfloat16
    tm = _pick_tile(seq, 512)
    tq = _pick_tile(seq, 512)
    tk = _pick_tile(tq, 512)
    tiles_per_seq = seq // tm
    x2 = x.reshape(n_tok, D_MODEL)

    wi = w_in[0]
    o = 0
    parts = {}
    for name, width in (("qa", HW), ("ka", HW), ("va", HW), ("qb", HW), ("kb", HW), ("vb", HW),
                        ("fb", FOX_HEADS), ("qc", HW), ("g", 3 * D_MODEL)):
        parts[name] = wi[:, o:o + width]
        o += width
    w_main = jnp.concatenate([parts[k] for k in ("qa", "ka", "va", "qb", "kb", "vb", "qc", "g")], axis=1).astype(bf)
    w_f = jnp.pad(parts["fb"], ((0, 0), (0, LANES - FOX_HEADS))).astype(bf)
    fb = jnp.pad(fox_forget_b[0], (0, LANES - FOX_HEADS)).reshape(1, LANES)
    tile_gain = lambda g, n: jnp.tile(g, n).reshape(1, -1)
    gains = (tile_gain(diff_qnorm_g[0], _NORM_CHUNK // HEAD_DIM), tile_gain(diff_knorm_g[0], _NORM_CHUNK // HEAD_DIM),
             tile_gain(fox_qnorm_g[0], _NORM_CHUNK // HEAD_DIM), tile_gain(fox_knorm_g[0], _NORM_CHUNK // HEAD_DIM),
             mem_qnorm_g[0].reshape(1, MEM_DIM))

    kmem, vmem = _mem_kv(mem.reshape(n_batch * mem_len, D_MODEL), mem_norm_g[0].reshape(1, D_MODEL),
                         w_mem_kv[0].astype(bf), mem_knorm_g[0].reshape(1, MEM_DIM), n_batch, mem_len)

    qa, ka, va, qb, kb, vb, qc, gate, cum_t = _in_proj(
        x2, attn_norm_g[0].reshape(1, D_MODEL), w_main, w_f, fb, positions.reshape(n_tok, 1).astype(jnp.int32),
        gains, tm, tiles_per_seq)

    yb = _fox_attn(qb, kb, vb, cum_t, n_batch, seq, tq, tk)
    ya = _diff_attn(qa, ka, va, diff_lambda[0], diff_subln_g[0].reshape(1, LANES), n_batch, seq, tq, tk)

    w_r = jnp.concatenate([w_router_group[0], w_router_expert[0]], axis=1)
    w_r = jnp.pad(w_r, ((0, 0), (0, LANES - w_r.shape[1])))
    r_hi = w_r.astype(bf)
    r_lo = (w_r - r_hi.astype(jnp.float32)).astype(bf)
    r_b = jnp.pad(jnp.concatenate([b_router_group[0], b_router_expert[0]]), (0, LANES - N_GROUPS - N_EXPERTS))
    x1, h2, wslab, cls3, rank3, counts = _merge(x2, ya, yb, qc, kmem, vmem, gate, w_o_diff[0].astype(bf), w_o_fox[0].astype(bf),
                            w_o_mem[0].astype(bf), w_out[0].astype(bf), ffn_norm_g[0].reshape(1, D_MODEL),
                            r_hi, r_lo, r_b.reshape(1, LANES), tm, tiles_per_seq, mem_len)

    blk = EXPERT_BLOCK
    n_blocks = -(-(n_tok + N_LIVE_CLASSES * (blk - 1)) // (blk * EXPERT_STEP_BLOCKS)) * EXPERT_STEP_BLOCKS
    counts = counts[0]
    padded = (counts + blk - 1) // blk * blk
    pends = jnp.cumsum(padded)
    pstarts = pends - padded
    blk_start = jnp.arange(n_blocks, dtype=jnp.int32) * blk
    blk_cls = jnp.minimum(jnp.sum((pends[None, :] <= blk_start[:, None]).astype(jnp.int32), axis=1),
                          N_CLASSES - 1).astype(jnp.int32)
    per_group = EXPERTS_PER_GROUP * EXPERTS_PER_GROUP
    blk_ea = blk_cls // EXPERTS_PER_GROUP
    blk_eb = (blk_cls // per_group) * EXPERTS_PER_GROUP + blk_cls % EXPERTS_PER_GROUP
    nused = (pends[-1] // blk).astype(jnp.int32).reshape(1)
    pstarts = pstarts.astype(jnp.int32)

    dest3 = _dest(pstarts, cls3, rank3)
    rows = _dispatch(pstarts, pends.astype(jnp.int32), dest3, h2, wslab, n_blocks * blk, tm)
    y = _experts(blk_ea, blk_eb, nused, rows, w_up[0].astype(bf), w_down[0].astype(bf))
    out = _combine(dest3, x1, y, tm)
    return out.reshape(x.shape)
```

```python
import functools
import math

import jax
import jax.numpy as jnp
from jax import lax
from jax.experimental import pallas as pl
from jax.experimental.pallas import tpu as pltpu

D_MODEL = 1024
HEAD_DIM = 64
DIFF_HEADS = 4
FOX_HEADS = 8
MEM_HEADS = 4
MEM_DIM = 128
ROPE_THETA = 500000.0
ROPE_DIM = HEAD_DIM // 4
N_GROUPS = 4
EXPERTS_PER_GROUP = 8
N_EXPERTS = N_GROUPS * EXPERTS_PER_GROUP
EXPERT_FF = 512
EPS = 1e-6
NEG_INF = -1e30
LAMBDA_INIT = 0.8 - 0.6 * math.exp(-0.3 * 0)
LOG2E = 1.4426950408889634

LANES = 128
VMEM_LIMIT_BYTES = 56 * 1024 * 1024

HW = 512
N_CLASSES = 256
N_LIVE_CLASSES = N_GROUPS * (EXPERTS_PER_GROUP * (EXPERTS_PER_GROUP - 1) // 2)
ROW_W = D_MODEL + LANES
EXPERT_BLOCK = 256


def _cparams(sem):
    return pltpu.CompilerParams(dimension_semantics=sem, vmem_limit_bytes=VMEM_LIMIT_BYTES)


def _rms_rows(x, g):
    return x * lax.rsqrt(jnp.mean(x * x, axis=-1, keepdims=True) + EPS) * g


def _dot(a, b):
    return jnp.dot(a, b, preferred_element_type=jnp.float32)


def _dot_nt(a, b):
    return lax.dot_general(a, b, (((1,), (1,)), ((), ())), preferred_element_type=jnp.float32)


def _mem_kv_kernel(mem_ref, g_ref, w_ref, kg_ref, k_out, v_out):
    mn = _rms_rows(mem_ref[...], g_ref[...]).astype(jnp.bfloat16)
    kv = _dot(mn, w_ref[...])
    for h in range(MEM_HEADS):
        kh = kv[:, h * MEM_DIM:(h + 1) * MEM_DIM]
        k_out[:, h * MEM_DIM:(h + 1) * MEM_DIM] = _rms_rows(kh, kg_ref[...]).astype(jnp.bfloat16)
    v_out[...] = kv[:, HW:].astype(jnp.bfloat16)


def _mem_kv(mem2, g, w_kv, kg, n_batch, mem_len):
    return pl.pallas_call(
        _mem_kv_kernel,
        grid=(n_batch,),
        in_specs=[
            pl.BlockSpec((mem_len, D_MODEL), lambda b: (b, 0)),
            pl.BlockSpec((1, D_MODEL), lambda b: (0, 0)),
            pl.BlockSpec((D_MODEL, 2 * HW), lambda b: (0, 0)),
            pl.BlockSpec((1, MEM_DIM), lambda b: (0, 0)),
        ],
        out_specs=[
            pl.BlockSpec((mem_len, HW), lambda b: (b, 0)),
            pl.BlockSpec((mem_len, HW), lambda b: (b, 0)),
        ],
        out_shape=[jax.ShapeDtypeStruct((n_batch * mem_len, HW), jnp.bfloat16)] * 2,
        compiler_params=_cparams(("arbitrary",)),
        name="mem_kv",
    )(mem2, g, w_kv, kg)


_C_QA, _C_KA, _C_VA, _C_QB, _C_KB, _C_VB, _C_QC, _C_G = (i * HW for i in range(8))
_MAIN_COLS = 7 * HW + 3 * D_MODEL
_NORM_CHUNK = 256
_HEADS_PER_CHUNK = _NORM_CHUNK // HEAD_DIM
FOX_SLAB = FOX_HEADS * LANES
BIAS_PARTS = 3


def _group_rms(r, bd, g):
    ss = _dot((r * r).astype(jnp.bfloat16), bd)
    return r * lax.rsqrt(ss * (1.0 / HEAD_DIM) + EPS) * g


def _in_proj_kernel(tiles_per_seq, x_ref, g_ref, w_ref, wf_ref, fb_ref, pos_ref, rinv_ref, ra_ref, rb_ref, bd_ref, tri_ref,
                    place_ref, gqa_ref, gka_ref, gqb_ref, gkb_ref, gqc_ref,
                    qa_o, ka_o, va_o, qb_o, kb_o, vb_o, qc_o, gate_o, cum_o, carry_ref):
    i = pl.program_id(0)

    @pl.when(i == 0)
    def _():
        carry_ref[...] = jnp.zeros_like(carry_ref)

    h = _rms_rows(x_ref[...], g_ref[...]).astype(jnp.bfloat16)
    bd = bd_ref[...]
    nchunk = HW // _NORM_CHUNK
    lane = lax.broadcasted_iota(jnp.int32, (h.shape[0], LANES), 1)

    def split3(v):
        p1 = v.astype(jnp.bfloat16)
        r1 = v - p1.astype(jnp.float32)
        p2 = r1.astype(jnp.bfloat16)
        return p1, p2, (r1 - p2.astype(jnp.float32)).astype(jnp.bfloat16)

    ang = pos_ref[...].astype(jnp.float32) * rinv_ref[...]
    sin = jnp.sin(ang)
    rep = _NORM_CHUNK // LANES
    rope_c = jnp.concatenate([jnp.cos(ang)] * rep, axis=1)
    rope_a = jnp.concatenate([sin * ra_ref[...]] * rep, axis=1)
    rope_b = jnp.concatenate([sin * rb_ref[...]] * rep, axis=1)
    half = ROPE_DIM // 2

    def rope(y):
        return (y * rope_c + pltpu.roll(y, _NORM_CHUNK - half, axis=1) * rope_a
                + pltpu.roll(y, half, axis=1) * rope_b)

    def normed_chunk(r, c, gain, use_rope, scale):
        y = _group_rms(r[:, c * _NORM_CHUNK:(c + 1) * _NORM_CHUNK], bd, gain)
        if use_rope:
            y = rope(y)
        return y if scale is None else y * scale

    def packed_branch(col0, out, gain_ref, scale):
        gain = gain_ref[...]
        r = _dot(h, w_ref[:, col0:col0 + HW])
        for c in range(nchunk):
            out[:, c * _NORM_CHUNK:(c + 1) * _NORM_CHUNK] = normed_chunk(r, c, gain, True, scale).astype(jnp.bfloat16)

    def slab_branch(col0, out, gain_ref, scale, spare):
        gain = gain_ref[...]
        r = _dot(h, w_ref[:, col0:col0 + HW])
        for c in range(nchunk):
            y = normed_chunk(r, c, gain, False, scale)
            for hh in range(_HEADS_PER_CHUNK):
                src = y[:, (hh // 2) * LANES:(hh // 2 + 1) * LANES]
                if hh % 2:
                    src = pltpu.roll(src, HEAD_DIM, axis=1)
                head = c * _HEADS_PER_CHUNK + hh
                out[:, head * LANES:(head + 1) * LANES] = jnp.where(lane < HEAD_DIM, src, spare(head)).astype(jnp.bfloat16)

    q_scale = HEAD_DIM ** -0.5 * LOG2E
    slab_branch(_C_QB, qb_o, gqb_ref, q_scale, lambda head: jnp.where(
        (lane >= HEAD_DIM) & (lane < HEAD_DIM + BIAS_PARTS), 1.0, 0.0))

    z = _dot(h, wf_ref[...]) + fb_ref[...]
    logf = -(jnp.maximum(-z, 0.0) + jnp.log1p(jnp.exp(-jnp.abs(z))))
    va_o[...] = _dot(h, w_ref[:, _C_VA:_C_VA + HW]).astype(jnp.bfloat16)
    vb_o[...] = _dot(h, w_ref[:, _C_VB:_C_VB + HW]).astype(jnp.bfloat16)

    tri = tri_ref[...]
    carry = jnp.where(i % tiles_per_seq == 0, 0.0, carry_ref[...])
    c3 = _dot(tri, jnp.concatenate(split3(logf), axis=1))
    cum = (c3[:, 0:LANES] + c3[:, LANES:2 * LANES]) + c3[:, 2 * LANES:3 * LANES] + carry
    cum_o[...] = cum.T[0:FOX_HEADS]
    carry_ref[...] = cum[cum.shape[0] - 1:, :]

    gqc = gqc_ref[...]
    rc = _dot(h, w_ref[:, _C_QC:_C_QC + HW])
    for hd in range(MEM_HEADS):
        sl = slice(hd * MEM_DIM, (hd + 1) * MEM_DIM)
        qc_o[:, sl] = (_rms_rows(rc[:, sl], gqc) * MEM_DIM ** -0.5).astype(jnp.bfloat16)

    packed_branch(_C_QA, qa_o, gqa_ref, q_scale)

    pieces = [p.astype(jnp.float32) for p in split3(jnp.where(lane < FOX_HEADS, cum * (-LOG2E), 0.0))]
    packed = pieces[0]
    for k in range(1, BIAS_PARTS):
        packed = packed + pltpu.roll(pieces[k], k * FOX_HEADS, axis=1)
    kbias = _dot(packed.astype(jnp.bfloat16), place_ref[...])
    packed_branch(_C_KA, ka_o, gka_ref, None)
    slab_branch(_C_KB, kb_o, gkb_ref, None, lambda head: kbias[:, head * LANES:(head + 1) * LANES])

    for c in range(3 * D_MODEL // HW):
        lo = _C_G + c * HW
        r = _dot(h, w_ref[:, lo:lo + HW])
        gate_o[:, c * HW:(c + 1) * HW] = jax.nn.sigmoid(r).astype(jnp.bfloat16)


def _in_proj(x2, g, w_main, w_f, fb, pos, gains, tm, tiles_per_seq):
    n_tok = x2.shape[0]
    half = ROPE_DIM // 2
    head_lane = jnp.arange(LANES, dtype=jnp.int32) % HEAD_DIM
    inv = ROPE_THETA ** (-jnp.arange(0, ROPE_DIM, 2, dtype=jnp.float32) / ROPE_DIM)
    rope_inv = jnp.where(head_lane < ROPE_DIM, inv[head_lane % half], 0.0).reshape(1, LANES)
    rope_sa = jnp.where(head_lane < half, -1.0, 0.0).reshape(1, LANES)
    rope_sb = jnp.where((head_lane >= half) & (head_lane < ROPE_DIM), 1.0, 0.0).reshape(1, LANES)
    bd = (lax.broadcasted_iota(jnp.int32, (_NORM_CHUNK, _NORM_CHUNK), 0) // HEAD_DIM
          == lax.broadcasted_iota(jnp.int32, (_NORM_CHUNK, _NORM_CHUNK), 1) // HEAD_DIM).astype(jnp.bfloat16)
    tri = (lax.broadcasted_iota(jnp.int32, (tm, tm), 1)
           <= lax.broadcasted_iota(jnp.int32, (tm, tm), 0)).astype(jnp.bfloat16)
    src = lax.broadcasted_iota(jnp.int32, (LANES, FOX_SLAB), 0)
    dst = lax.broadcasted_iota(jnp.int32, (LANES, FOX_SLAB), 1)
    place = ((src < BIAS_PARTS * FOX_HEADS)
             & (dst == (src % FOX_HEADS) * LANES + HEAD_DIM + src // FOX_HEADS)).astype(jnp.bfloat16)
    const = lambda shape: pl.BlockSpec(shape, lambda i: (0, 0))
    resident = lambda shape: pl.BlockSpec(shape, lambda i: (0, 0), pipeline_mode=pl.Buffered(1))
    row = lambda w: pl.BlockSpec((tm, w), lambda i: (i, 0))
    slab = lambda w: jax.ShapeDtypeStruct((n_tok, w), jnp.bfloat16)
    return pl.pallas_call(
        functools.partial(_in_proj_kernel, tiles_per_seq),
        grid=(n_tok // tm,),
        in_specs=[row(D_MODEL), const((1, D_MODEL)), resident((D_MODEL, _MAIN_COLS)), const((D_MODEL, LANES)),
                  const((1, LANES)), row(1), const((1, LANES)), const((1, LANES)), const((1, LANES)),
                  const((_NORM_CHUNK, _NORM_CHUNK)), const((tm, tm)), const((LANES, FOX_SLAB)),
                  const((1, _NORM_CHUNK)), const((1, _NORM_CHUNK)), const((1, _NORM_CHUNK)),
                  const((1, _NORM_CHUNK)), const((1, MEM_DIM))],
        out_specs=[row(HW), row(HW), row(HW), row(FOX_SLAB), row(FOX_SLAB), row(HW), row(HW),
                   row(3 * D_MODEL), pl.BlockSpec((FOX_HEADS, tm), lambda i: (0, i))],
        out_shape=[slab(HW), slab(HW), slab(HW), slab(FOX_SLAB), slab(FOX_SLAB), slab(HW), slab(HW),
                   slab(3 * D_MODEL), jax.ShapeDtypeStruct((FOX_HEADS, n_tok), jnp.float32)],
        scratch_shapes=[pltpu.VMEM((1, LANES), jnp.float32)],
        compiler_params=_cparams(("arbitrary",)),
        name="in_proj",
    )(x2, g, w_main, w_f, fb, pos, rope_inv, rope_sa, rope_sb, bd, tri, place, *gains)


FOX_V_ROWS = 80
DIFF_V_ROWS = 144


Q_SPLIT = 2


def _flash_transposed(qs, k_of, v_of, shifts, tq, v_rows, finish):
    qi = pl.program_id(2)
    tqp = tq // Q_SPLIT
    parts = [(w, h) for w in range(len(qs)) for h in range(Q_SPLIT)]
    q_parts = [qs[w][h * tqp:(h + 1) * tqp] for (w, h) in parts]
    shift_parts = [0.0 if shifts[w] is None else shifts[w][:, h * tqp:(h + 1) * tqp] for (w, h) in parts]

    def block(k0, n, diag, state):
        def n_keys(h):
            return n if diag is None else min(n, diag + (h + 1) * tqp)

        sc = []
        for c, (w, h) in enumerate(parts):
            s = _dot_nt(k_of(w, k0, n_keys(h)), q_parts[c])
            if diag is not None:
                key = lax.broadcasted_iota(jnp.int32, s.shape, 0) - diag
                qry = lax.broadcasted_iota(jnp.int32, s.shape, 1) + h * tqp
                s = jnp.where(key <= qry, s, NEG_INF)
            sc.append((s, jnp.max(s, axis=0, keepdims=True)))
        out = []
        for c, (w, h) in enumerate(parts):
            (s, s_max), (m_old, acc_old), shift = sc[c], state[c], shift_parts[c]
            m_new = jnp.maximum(m_old, s_max + shift)
            p = jnp.exp2(s - (m_new - shift))
            acc_new = jnp.exp2(m_old - m_new) * acc_old + _dot(v_of(w, k0, n_keys(h)), p.astype(jnp.bfloat16))
            out.append((m_new, acc_new))
        return tuple(out)

    def done(state):
        finish([jnp.concatenate([state[w * Q_SPLIT + h][1] for h in range(Q_SPLIT)], axis=1)
                for w in range(len(qs))])

    init = tuple((jnp.full((1, tqp), NEG_INF, jnp.float32), jnp.zeros((v_rows, tqp), jnp.float32)) for _ in parts)
    state = lax.fori_loop(0, qi // 2, lambda j, st: block(pl.multiple_of(j * 2 * tq, 2 * tq), 2 * tq, None, st), init)

    @pl.when(qi % 2 == 1)
    def _():
        done(block(pl.multiple_of((qi - 1) * tq, tq), 2 * tq, tq, state))

    @pl.when(qi % 2 == 0)
    def _():
        done(block(pl.multiple_of(qi * tq, tq), tq, 0, state))


ATTN_CHAINS = 4


def _fill_values_t(v_ref, vt_ref, head_w, tk):
    n_heads, v_rows, seq = vt_ref.shape
    tail = lax.broadcasted_iota(jnp.int32, (v_rows - head_w, seq), 0)
    ones_row = jnp.where(tail == 0, 1.0, 0.0).astype(jnp.bfloat16)
    for h in range(n_heads):
        vt_ref[h, head_w:v_rows, :] = ones_row
    for c in range(seq // tk):
        blk_t = v_ref[c * tk:(c + 1) * tk, :].astype(jnp.float32).T
        for h in range(n_heads):
            vt_ref[h, 0:head_w, c * tk:(c + 1) * tk] = blk_t[h * head_w:(h + 1) * head_w].astype(jnp.bfloat16)

FOX_STEP_HEADS = ATTN_CHAINS
DIFF_STEP_HEADS = ATTN_CHAINS // 2


def _fox_kernel(tq, tk, q_ref, k_ref, v_ref, cq_ref, o_ref, vt_ref):
    heads = range(FOX_STEP_HEADS)

    @pl.when(pl.program_id(2) == 0)
    def _():
        _fill_values_t(v_ref, vt_ref, HEAD_DIM, tk)

    qs = [q_ref[:, w * LANES:(w + 1) * LANES] for w in heads]
    head0 = pl.program_id(1) * FOX_STEP_HEADS
    shifts = [cq_ref[pl.ds(head0 + w, 1), :] * LOG2E for w in heads]

    def finish(accs):
        o_t = jnp.concatenate([a[0:HEAD_DIM] / a[HEAD_DIM:HEAD_DIM + 1] for a in accs], axis=0)
        o_ref[...] = o_t.T.astype(jnp.bfloat16)

    _flash_transposed(qs, lambda w, k0, n: k_ref[pl.ds(k0, n), w * LANES:(w + 1) * LANES],
                      lambda w, k0, n: vt_ref[w, :, pl.ds(k0, n)], shifts, tq, FOX_V_ROWS, finish)


def _fox_attn(qb, kb, vb, cum_t, n_batch, seq, tq, tk):
    nq = seq // tq
    hs = FOX_STEP_HEADS
    return pl.pallas_call(
        functools.partial(_fox_kernel, tq, tk),
        grid=(n_batch, FOX_HEADS // hs, nq),
        in_specs=[
            pl.BlockSpec((tq, hs * LANES), lambda b, j, i: (b * nq + i, j)),
            pl.BlockSpec((seq, hs * LANES), lambda b, j, i: (b, j)),
            pl.BlockSpec((seq, hs * HEAD_DIM), lambda b, j, i: (b, j)),
            pl.BlockSpec((FOX_HEADS, tq), lambda b, j, i: (0, b * nq + i)),
        ],
        out_specs=pl.BlockSpec((tq, hs * HEAD_DIM), lambda b, j, i: (b * nq + i, j)),
        out_shape=jax.ShapeDtypeStruct((n_batch * seq, HW), jnp.bfloat16),
        scratch_shapes=[pltpu.VMEM((hs, FOX_V_ROWS, seq), jnp.bfloat16)],
        compiler_params=_cparams(("arbitrary", "arbitrary", "arbitrary")),
        name="fox_attn",
    )(qb, kb, vb, cum_t)


def _diff_kernel(tq, tk, q_ref, k_ref, v_ref, lam_ref, g_ref, o_ref, vt_ref):
    @pl.when(pl.program_id(2) == 0)
    def _():
        _fill_values_t(v_ref, vt_ref, 2 * HEAD_DIM, tk)

    lp = lam_ref[...]
    lam = (jnp.exp(jnp.sum(lp[0:1] * lp[1:2], axis=1, keepdims=True))
           - jnp.exp(jnp.sum(lp[2:3] * lp[3:4], axis=1, keepdims=True)) + LAMBDA_INIT)
    lane = lax.broadcasted_iota(jnp.int32, (tq, LANES), 1)
    qs = []
    for hd in range(DIFF_STEP_HEADS):
        q = q_ref[:, hd * LANES:(hd + 1) * LANES]
        zero = jnp.zeros_like(q)
        qs += [jnp.where(lane < HEAD_DIM, q, zero), jnp.where(lane >= HEAD_DIM, q, zero)]
    dv = 2 * HEAD_DIM

    def finish(accs):
        outs = []
        for hd in range(DIFF_STEP_HEADS):
            acc1, acc2 = accs[2 * hd], accs[2 * hd + 1]
            o_t = acc1[0:dv] / acc1[dv:dv + 1] - lam * (acc2[0:dv] / acc2[dv:dv + 1])
            outs.append(o_t * lax.rsqrt(jnp.mean(o_t * o_t, axis=0, keepdims=True) + EPS))
        gain = jnp.concatenate([g_ref[...]] * DIFF_STEP_HEADS, axis=1)
        o_ref[...] = (jnp.concatenate(outs, axis=0).T * gain * (1.0 - LAMBDA_INIT)).astype(jnp.bfloat16)

    _flash_transposed(qs, lambda w, k0, n: k_ref[pl.ds(k0, n), (w // 2) * LANES:(w // 2 + 1) * LANES],
                      lambda w, k0, n: vt_ref[w // 2, :, pl.ds(k0, n)], [None] * len(qs), tq, DIFF_V_ROWS, finish)


def _diff_attn(qa, ka, va, lam_params, subln_g, n_batch, seq, tq, tk):
    nq = seq // tq
    hs = DIFF_STEP_HEADS
    return pl.pallas_call(
        functools.partial(_diff_kernel, tq, tk),
        grid=(n_batch, DIFF_HEADS // hs, nq),
        in_specs=[
            pl.BlockSpec((tq, hs * LANES), lambda b, j, i: (b * nq + i, j)),
            pl.BlockSpec((seq, hs * LANES), lambda b, j, i: (b, j)),
            pl.BlockSpec((seq, hs * LANES), lambda b, j, i: (b, j)),
            pl.BlockSpec((4, HEAD_DIM), lambda b, j, i: (0, 0)),
            pl.BlockSpec((1, LANES), lambda b, j, i: (0, 0)),
        ],
        out_specs=pl.BlockSpec((tq, hs * LANES), lambda b, j, i: (b * nq + i, j)),
        out_shape=jax.ShapeDtypeStruct(qa.shape, jnp.bfloat16),
        scratch_shapes=[pltpu.VMEM((hs, DIFF_V_ROWS, seq), jnp.bfloat16)],
        compiler_params=_cparams(("arbitrary", "arbitrary", "arbitrary")),
        name="diff_attn",
    )(qa, ka, va, lam_params, subln_g)


def _merge_kernel(x_ref, ya_ref, yb_ref, qc_ref, km_ref, vm_ref, gate_ref, wd_ref, wf_ref, wm_ref, wo_ref,
                  fg_ref, rhi_ref, rlo_ref, rb_ref, tri_ref, x1_o, h2_o, w_o, cls_o, rank_o, cnt_o, carry_ref, lg_ref):
    i = pl.program_id(0)

    @pl.when(i == 0)
    def _():
        carry_ref[...] = jnp.zeros_like(carry_ref)
        lg_ref[...] = jnp.zeros_like(lg_ref)

    lg_prev = lg_ref[(i + 1) % 2]

    heads = []
    for h in range(MEM_HEADS):
        sl = slice(h * MEM_DIM, (h + 1) * MEM_DIM)
        s = _dot_nt(qc_ref[:, sl], km_ref[:, sl])
        p = jnp.exp(s - jnp.max(s, axis=1, keepdims=True))
        o = _dot(p.astype(jnp.bfloat16), vm_ref[:, sl]) / jnp.sum(p, axis=1, keepdims=True)
        heads.append(o.astype(jnp.bfloat16))
    yc = jnp.concatenate(heads, axis=1)

    merged = (gate_ref[:, 0:D_MODEL].astype(jnp.float32) * _dot(ya_ref[...], wd_ref[...])
              + gate_ref[:, D_MODEL:2 * D_MODEL].astype(jnp.float32) * _dot(yb_ref[...], wf_ref[...])
              + gate_ref[:, 2 * D_MODEL:3 * D_MODEL].astype(jnp.float32) * _dot(yc, wm_ref[...]))
    _route_tile(lg_prev, i > 0, tri_ref, w_o, cls_o, rank_o, cnt_o, carry_ref)
    x1 = x_ref[...] + _dot(merged.astype(jnp.bfloat16), wo_ref[...])
    x1_o[...] = x1
    h2 = _rms_rows(x1, fg_ref[...])
    h2_o[...] = h2
    a_hi = h2.astype(jnp.bfloat16)
    a_lo = (h2 - a_hi.astype(jnp.float32)).astype(jnp.bfloat16)
    lg_ref[i % 2] = (_dot(a_hi, rhi_ref[...]) + _dot(a_lo, rhi_ref[...]) + _dot(a_hi, rlo_ref[...])) + rb_ref[...]


def _merge(x2, ya, yb, qc, kmem, vmem, gate, wd, wf, wm, wo, fg, rhi, rlo, rb, tm, tiles_per_seq, mem_len):
    n_tok = x2.shape[0]
    nt = n_tok // tm
    cur = lambda i: jnp.minimum(i, nt - 1)
    prev = lambda i: jnp.maximum(i - 1, 0)
    const = lambda shape: pl.BlockSpec(shape, lambda i: (0, 0))
    row = lambda w: pl.BlockSpec((tm, w), lambda i: (cur(i), 0))
    memspec = pl.BlockSpec((mem_len, HW), lambda i: (cur(i) // tiles_per_seq, 0))
    tri = (lax.broadcasted_iota(jnp.int32, (tm, tm), 1)
           < lax.broadcasted_iota(jnp.int32, (tm, tm), 0)).astype(jnp.bfloat16)
    idx_row = pl.BlockSpec((1, 1, tm), lambda i: (prev(i), 0, 0))
    return pl.pallas_call(
        _merge_kernel,
        grid=(nt + 1,),
        in_specs=[row(D_MODEL), row(HW), row(HW), row(HW), memspec, memspec, row(3 * D_MODEL),
                  const((HW, D_MODEL)), const((HW, D_MODEL)), const((HW, D_MODEL)), const((D_MODEL, D_MODEL)),
                  const((1, D_MODEL)), const((D_MODEL, LANES)), const((D_MODEL, LANES)), const((1, LANES)),
                  const((tm, tm))],
        out_specs=[row(D_MODEL), row(D_MODEL), pl.BlockSpec((tm, LANES), lambda i: (prev(i), 0)), idx_row, idx_row,
                   const((1, N_CLASSES))],
        out_shape=[jax.ShapeDtypeStruct((n_tok, D_MODEL), jnp.float32),
                   jax.ShapeDtypeStruct((n_tok, D_MODEL), jnp.float32),
                   jax.ShapeDtypeStruct((n_tok, LANES), jnp.float32),
                   jax.ShapeDtypeStruct((nt, 1, tm), jnp.int32),
                   jax.ShapeDtypeStruct((nt, 1, tm), jnp.int32),
                   jax.ShapeDtypeStruct((1, N_CLASSES), jnp.int32)],
        scratch_shapes=[pltpu.VMEM((1, N_CLASSES), jnp.float32), pltpu.VMEM((2, tm, LANES), jnp.float32)],
        compiler_params=_cparams(("arbitrary",)),
        name="merge",
    )(x2, ya, yb, qc, kmem, vmem, gate, wd, wf, wm, wo, fg, rhi, rlo, rb, tri)


_EXPERT_LANE0 = N_GROUPS


def _route_tile(lg, counted, tri_ref, w_o, cls_o, rank_o, cnt_o, carry_ref):
    lanei = lax.broadcasted_iota(jnp.int32, lg.shape, 1)
    lane = lanei.astype(jnp.float32)
    ninf = jnp.float32(-jnp.inf)
    big = jnp.float32(LANES)

    def first_max(vals):
        top = jnp.max(vals, axis=1, keepdims=True)
        return top, jnp.min(jnp.where(vals == top, lane, big), axis=1, keepdims=True)

    is_group = lanei < N_GROUPS
    gmax, gidx = first_max(jnp.where(is_group, lg, ninf))
    gate = 1.0 / jnp.sum(jnp.where(is_group, jnp.exp(lg - gmax), 0.0), axis=1, keepdims=True)
    lo = _EXPERT_LANE0 + EXPERTS_PER_GROUP * gidx
    el = jnp.where((lane >= lo) & (lane < lo + EXPERTS_PER_GROUP), lg, ninf)
    t1, i1 = first_max(el)
    t2, i2 = first_max(jnp.where(lane == i1, ninf, el))
    e2 = jnp.exp(t2 - t1)
    w1 = gate * (1.0 / (1.0 + e2))
    w2 = gate * (e2 / (1.0 + e2))
    first_low = i1 < i2
    ea = jnp.minimum(i1, i2) - lo
    eb = jnp.maximum(i1, i2) - lo
    wa = jnp.where(first_low, w1, w2)
    wb = jnp.where(first_low, w2, w1)
    cls = (EXPERTS_PER_GROUP * EXPERTS_PER_GROUP) * gidx + EXPERTS_PER_GROUP * ea + eb

    cls_lane = lax.broadcasted_iota(jnp.int32, (lg.shape[0], N_CLASSES), 1).astype(jnp.float32)
    onehot = cls_lane == cls
    before = _dot(tri_ref[...], jnp.where(onehot, 1.0, 0.0).astype(jnp.bfloat16)) + carry_ref[...]
    rank = jnp.sum(jnp.where(onehot, before, 0.0), axis=1, keepdims=True)
    carry_ref[...] = carry_ref[...] + jnp.where(counted, jnp.sum(jnp.where(onehot, 1.0, 0.0), axis=0, keepdims=True), 0.0)
    cnt_o[...] = carry_ref[...].astype(jnp.int32)

    w_o[...] = jnp.where(lanei == 0, wa, jnp.where(lanei == 1, wb, 0.0))
    idx_t = jnp.where(lanei == 0, cls, jnp.where(lanei == 1, rank, 0.0)).T
    cls_o[0] = idx_t[0:1, :].astype(jnp.int32)
    rank_o[0] = idx_t[1:2, :].astype(jnp.int32)


ROW_DMA_UNROLL = 8


def _dest_kernel(cls_ref, rank_ref, ps_ref, dest_o):
    sub = lax.broadcasted_iota(jnp.int32, (N_CLASSES, cls_ref.shape[2]), 0).astype(jnp.float32)
    for t in range(cls_ref.shape[0]):
        cls = cls_ref[t].astype(jnp.float32)
        start = jnp.sum(jnp.where(sub == cls, ps_ref[...], 0.0), axis=0, keepdims=True)
        dest_o[t] = (start + rank_ref[t].astype(jnp.float32)).astype(jnp.int32)


def _dest(pstarts, cls3, rank3):
    nt, _, tm = cls3.shape
    tiles = _pick_tile(nt, 8)
    idx = pl.BlockSpec((tiles, 1, tm), lambda i: (i, 0, 0))
    return pl.pallas_call(
        _dest_kernel,
        grid=(nt // tiles,),
        in_specs=[idx, idx, pl.BlockSpec((N_CLASSES, 1), lambda i: (0, 0))],
        out_specs=idx,
        out_shape=jax.ShapeDtypeStruct(cls3.shape, jnp.int32),
        compiler_params=_cparams(("arbitrary",)),
        name="dest",
    )(cls3, rank3, pstarts.astype(jnp.float32).reshape(N_CLASSES, 1))


def _for_rows(tm, fn):
    def trip(g, c):
        for u in range(ROW_DMA_UNROLL):
            fn(g, u)
        return c

    lax.fori_loop(0, tm // ROW_DMA_UNROLL, trip, 0)


def _dispatch_kernel(tm, pstart_ref, pend_ref, dest_ref, pdest_ref, h2_ref, w_ref, rows_out,
                     tile_ref, zero_ref, sems):
    i = pl.program_id(0)
    slot = i % 2

    def row_copy(d_ref, s, g, u):
        dest = d_ref[0, 0, g * ROW_DMA_UNROLL + u]
        return pltpu.make_async_copy(tile_ref.at[s, g, pl.ds(u, 1)], rows_out.at[pl.ds(dest, 1)], sems.at[s])

    @pl.when(i == 0)
    def _():
        zero_ref[...] = jnp.zeros_like(zero_ref)

        def zero_copy(c):
            start = pl.multiple_of(pend_ref[c] - EXPERT_BLOCK, EXPERT_BLOCK)
            return pltpu.make_async_copy(zero_ref, rows_out.at[pl.ds(start, EXPERT_BLOCK)], sems.at[2])

        def for_spans(fn):
            def trip(c, carry):
                @pl.when(pend_ref[c] > pstart_ref[c])
                def _():
                    fn(c)
                return carry

            lax.fori_loop(0, N_CLASSES, trip, 0)

        def tail_copy(b):
            start = pl.multiple_of(b * EXPERT_BLOCK, EXPERT_BLOCK)
            return pltpu.make_async_copy(zero_ref, rows_out.at[pl.ds(start, EXPERT_BLOCK)], sems.at[2])

        def for_tail(fn):
            def trip(b, carry):
                fn(b)
                return carry

            lax.fori_loop(pend_ref[N_CLASSES - 1] // EXPERT_BLOCK, rows_out.shape[0] // EXPERT_BLOCK, trip, 0)

        for_spans(lambda c: zero_copy(c).start())
        for_tail(lambda b: tail_copy(b).start())
        for_spans(lambda c: zero_copy(c).wait())
        for_tail(lambda b: tail_copy(b).wait())

    groups = tm // ROW_DMA_UNROLL
    tile_ref[slot, :, :, 0:D_MODEL] = h2_ref[...].reshape(groups, ROW_DMA_UNROLL, D_MODEL)
    tile_ref[slot, :, :, D_MODEL:ROW_W] = w_ref[...].reshape(groups, ROW_DMA_UNROLL, LANES)
    _for_rows(tm, lambda g, u: row_copy(dest_ref, slot, g, u).start(priority=u % 2))

    @pl.when(i > 0)
    def _():
        _for_rows(tm, lambda g, u: row_copy(pdest_ref, 1 - slot, g, u).wait())

    @pl.when(i == pl.num_programs(0) - 1)
    def _():
        _for_rows(tm, lambda g, u: row_copy(dest_ref, slot, g, u).wait())


def _dispatch(pstarts, pends, dest3, h2, wslab, n_rows, tm):
    n_tok = h2.shape[0]
    idx = lambda f: pl.BlockSpec((1, 1, tm), f, memory_space=pltpu.SMEM)
    grid_spec = pltpu.PrefetchScalarGridSpec(
        num_scalar_prefetch=2,
        grid=(n_tok // tm,),
        in_specs=[idx(lambda i, ps, pe: (i, 0, 0)), idx(lambda i, ps, pe: (jnp.maximum(i - 1, 0), 0, 0)),
                  pl.BlockSpec((tm, D_MODEL), lambda i, ps, pe: (i, 0)),
                  pl.BlockSpec((tm, LANES), lambda i, ps, pe: (i, 0))],
        out_specs=pl.BlockSpec(memory_space=pl.ANY),
        scratch_shapes=[pltpu.VMEM((2, tm // ROW_DMA_UNROLL, ROW_DMA_UNROLL, ROW_W), jnp.float32),
                        pltpu.VMEM((EXPERT_BLOCK, ROW_W), jnp.float32),
                        pltpu.SemaphoreType.DMA((3,))],
    )
    return pl.pallas_call(
        functools.partial(_dispatch_kernel, tm),
        grid_spec=grid_spec,
        out_shape=jax.ShapeDtypeStruct((n_rows, ROW_W), jnp.float32),
        compiler_params=_cparams(("arbitrary",)),
        name="dispatch",
    )(pstarts, pends, dest3, dest3, h2, wslab)


def _combine_kernel(tm, dest_ref, ndest_ref, x1_ref, y_ref, o_ref, buf_ref, sems):
    i = pl.program_id(0)
    slot = i % 2

    def row_copy(d_ref, s, g, u):
        src = d_ref[0, 0, g * ROW_DMA_UNROLL + u]
        return pltpu.make_async_copy(y_ref.at[pl.ds(src, 1)], buf_ref.at[s, g, pl.ds(u, 1)], sems.at[s])

    @pl.when(i == 0)
    def _():
        _for_rows(tm, lambda g, u: row_copy(dest_ref, slot, g, u).start(priority=u % 2))

    @pl.when(i < pl.num_programs(0) - 1)
    def _():
        _for_rows(tm, lambda g, u: row_copy(ndest_ref, 1 - slot, g, u).start(priority=u % 2))

    _for_rows(tm, lambda g, u: row_copy(dest_ref, slot, g, u).wait())
    o_ref[...] = x1_ref[...] + buf_ref[slot].reshape(tm, D_MODEL)


def _combine(dest3, x1, y, tm):
    n_tok = x1.shape[0]
    nt = n_tok // tm
    idx = lambda f: pl.BlockSpec((1, 1, tm), f, memory_space=pltpu.SMEM)
    return pl.pallas_call(
        functools.partial(_combine_kernel, tm),
        grid=(nt,),
        in_specs=[idx(lambda i: (i, 0, 0)), idx(lambda i: (jnp.minimum(i + 1, nt - 1), 0, 0)),
                  pl.BlockSpec((tm, D_MODEL), lambda i: (i, 0)),
                  pl.BlockSpec(memory_space=pl.ANY)],
        out_specs=pl.BlockSpec((tm, D_MODEL), lambda i: (i, 0)),
        out_shape=jax.ShapeDtypeStruct(x1.shape, jnp.float32),
        scratch_shapes=[pltpu.VMEM((2, tm // ROW_DMA_UNROLL, ROW_DMA_UNROLL, D_MODEL), jnp.float32),
                        pltpu.SemaphoreType.DMA((2,))],
        compiler_params=_cparams(("arbitrary",)),
        name="combine",
    )(dest3, dest3, x1, y)


EXPERT_STEP_BLOCKS = 2


def _expert_kernel(ea_ref, eb_ref, nused_ref, rows_ref, *refs):
    y_ref = refs[-1]
    step = pl.program_id(0)

    @pl.when(step * EXPERT_STEP_BLOCKS < nused_ref[0])
    def _():
        for j in range(EXPERT_STEP_BLOCKS):
            ua_ref, ub_ref, da_ref, db_ref = refs[4 * j:4 * j + 4]
            rows = slice(j * EXPERT_BLOCK, (j + 1) * EXPERT_BLOCK)
            h = rows_ref[rows, 0:D_MODEL].astype(jnp.bfloat16)

            def expert(up_ref, down_ref, w):
                u = _dot(h, up_ref[0])
                act = (jax.nn.silu(u[:, 0:EXPERT_FF]) * u[:, EXPERT_FF:]).astype(jnp.bfloat16)
                return w * _dot(act, down_ref[0])

            y_ref[rows, :] = (expert(ua_ref, da_ref, rows_ref[rows, D_MODEL:D_MODEL + 1])
                              + expert(ub_ref, db_ref, rows_ref[rows, D_MODEL + 1:D_MODEL + 2]))

    @pl.when(step * EXPERT_STEP_BLOCKS >= nused_ref[0])
    def _():
        y_ref[...] = jnp.zeros_like(y_ref)


def _experts(blk_ea, blk_eb, nused, rows, w_up, w_down):
    n_blocks = rows.shape[0] // EXPERT_BLOCK
    nsb = EXPERT_STEP_BLOCKS
    assert n_blocks % nsb == 0
    step_rows = nsb * EXPERT_BLOCK
    last_step = lambda nu: (nu[0] - 1) // nsb
    weight_specs = []
    for j in range(nsb):
        weight_specs += [
            pl.BlockSpec((1, D_MODEL, 2 * EXPERT_FF), lambda s, ea, eb, nu, j=j: (ea[nsb * s + j], 0, 0)),
            pl.BlockSpec((1, D_MODEL, 2 * EXPERT_FF), lambda s, ea, eb, nu, j=j: (eb[nsb * s + j], 0, 0)),
            pl.BlockSpec((1, EXPERT_FF, D_MODEL), lambda s, ea, eb, nu, j=j: (ea[nsb * s + j], 0, 0)),
            pl.BlockSpec((1, EXPERT_FF, D_MODEL), lambda s, ea, eb, nu, j=j: (eb[nsb * s + j], 0, 0)),
        ]
    grid_spec = pltpu.PrefetchScalarGridSpec(
        num_scalar_prefetch=3,
        grid=(n_blocks // nsb,),
        in_specs=[pl.BlockSpec((step_rows, ROW_W), lambda s, ea, eb, nu: (jnp.minimum(s, last_step(nu)), 0))]
        + weight_specs,
        out_specs=pl.BlockSpec((step_rows, D_MODEL), lambda s, ea, eb, nu: (s, 0)),
    )
    weights = [w_up, w_up, w_down, w_down] * nsb
    return pl.pallas_call(
        _expert_kernel,
        grid_spec=grid_spec,
        out_shape=jax.ShapeDtypeStruct((rows.shape[0], D_MODEL), jnp.float32),
        compiler_params=_cparams(("arbitrary",)),
        name="experts",
    )(blk_ea, blk_eb, nused, rows, *weights)


def _pick_tile(n, pref):
    t = min(n, pref)
    while n % t:
        t //= 2
    return t


def kernel(x, mem, positions, attn_norm_g, w_in, diff_qnorm_g, diff_knorm_g, diff_lambda, diff_subln_g,
           fox_qnorm_g, fox_knorm_g, fox_forget_b, mem_norm_g, w_mem_kv, mem_qnorm_g, mem_knorm_g, w_o_diff,
           w_o_fox, w_o_mem, w_out, ffn_norm_g, w_router_group, b_router_group, w_router_expert,
           b_router_expert, w_up, w_down):
    n_batch, seq, _ = x.shape
    mem_len = mem.shape[1]
    n_tok = n_batch * seq
    bf = jnp.bfloat16
    tm = _pick_tile(seq, 512)
    tq = _pick_tile(seq, 512)
    tk = _pick_tile(tq, 512)
    tiles_per_seq = seq // tm
    x2 = x.reshape(n_tok, D_MODEL)

    wi = w_in[0]
    o = 0
    parts = {}
    for name, width in (("qa", HW), ("ka", HW), ("va", HW), ("qb", HW), ("kb", HW), ("vb", HW),
                        ("fb", FOX_HEADS), ("qc", HW), ("g", 3 * D_MODEL)):
        parts[name] = wi[:, o:o + width]
        o += width
    w_main = jnp.concatenate([parts[k] for k in ("qa", "ka", "va", "qb", "kb", "vb", "qc", "g")], axis=1).astype(bf)
    w_f = jnp.pad(parts["fb"], ((0, 0), (0, LANES - FOX_HEADS))).astype(bf)
    fb = jnp.pad(fox_forget_b[0], (0, LANES - FOX_HEADS)).reshape(1, LANES)
    tile_gain = lambda g, n: jnp.tile(g, n).reshape(1, -1)
    gains = (tile_gain(diff_qnorm_g[0], _NORM_CHUNK // HEAD_DIM), tile_gain(diff_knorm_g[0], _NORM_CHUNK // HEAD_DIM),
             tile_gain(fox_qnorm_g[0], _NORM_CHUNK // HEAD_DIM), tile_gain(fox_knorm_g[0], _NORM_CHUNK // HEAD_DIM),
             mem_qnorm_g[0].reshape(1, MEM_DIM))

    kmem, vmem = _mem_kv(mem.reshape(n_batch * mem_len, D_MODEL), mem_norm_g[0].reshape(1, D_MODEL),
                         w_mem_kv[0].astype(bf), mem_knorm_g[0].reshape(1, MEM_DIM), n_batch, mem_len)

    qa, ka, va, qb, kb, vb, qc, gate, cum_t = _in_proj(
        x2, attn_norm_g[0].reshape(1, D_MODEL), w_main, w_f, fb, positions.reshape(n_tok, 1).astype(jnp.int32),
        gains, tm, tiles_per_seq)

    yb = _fox_attn(qb, kb, vb, cum_t, n_batch, seq, tq, tk)
    ya = _diff_attn(qa, ka, va, diff_lambda[0], diff_subln_g[0].reshape(1, LANES), n_batch, seq, tq, tk)

    w_r = jnp.concatenate([w_router_group[0], w_router_expert[0]], axis=1)
    w_r = jnp.pad(w_r, ((0, 0), (0, LANES - w_r.shape[1])))
    r_hi = w_r.astype(bf)
    r_lo = (w_r - r_hi.astype(jnp.float32)).astype(bf)
    r_b = jnp.pad(jnp.concatenate([b_router_group[0], b_router_expert[0]]), (0, LANES - N_GROUPS - N_EXPERTS))
    x1, h2, wslab, cls3, rank3, counts = _merge(x2, ya, yb, qc, kmem, vmem, gate, w_o_diff[0].astype(bf), w_o_fox[0].astype(bf),
                            w_o_mem[0].astype(bf), w_out[0].astype(bf), ffn_norm_g[0].reshape(1, D_MODEL),
                            r_hi, r_lo, r_b.reshape(1, LANES), tm, tiles_per_seq, mem_len)

    blk = EXPERT_BLOCK
    n_blocks = -(-(n_tok + N_LIVE_CLASSES * (blk - 1)) // (blk * EXPERT_STEP_BLOCKS)) * EXPERT_STEP_BLOCKS
    counts = counts[0]
    padded = (counts + blk - 1) // blk * blk
    pends = jnp.cumsum(padded)
    pstarts = pends - padded
    blk_start = jnp.arange(n_blocks, dtype=jnp.int32) * blk
    blk_cls = jnp.minimum(jnp.sum((pends[None, :] <= blk_start[:, None]).astype(jnp.int32), axis=1),
                          N_CLASSES - 1).astype(jnp.int32)
    per_group = EXPERTS_PER_GROUP * EXPERTS_PER_GROUP
    blk_ea = blk_cls // EXPERTS_PER_GROUP
    blk_eb = (blk_cls // per_group) * EXPERTS_PER_GROUP + blk_cls % EXPERTS_PER_GROUP
    nused = (pends[-1] // blk).astype(jnp.int32).reshape(1)
    pstarts = pstarts.astype(jnp.int32)

    dest3 = _dest(pstarts, cls3, rank3)
    rows = _dispatch(pstarts, pends.astype(jnp.int32), dest3, h2, wslab, n_blocks * blk, tm)
    y = _experts(blk_ea, blk_eb, nused, rows, w_up[0].astype(bf), w_down[0].astype(bf))
    out = _combine(dest3, x1, y, tm)
    return out.reshape(x.shape)
```
